```python
import jax, jax.numpy as jnp
from jax import lax
import numpy as np

D_MODEL = 1024
BATCH = 4
SEQ = 4096
DEPTH = 1

D_PLE = 256
D_MIX = 2 * D_MODEL
D_SSM = D_MIX // 2
D_CONF = D_MIX - D_SSM
SSM_HEAD_DIM = 64
SSM_HEADS = D_SSM // SSM_HEAD_DIM
SSM_GROUPS = 2
SSM_HPG = SSM_HEADS // SSM_GROUPS
SSM_STATE = 128
SSM_CONV_K = 4
CHUNK = 128
CONF_K = 31
LN_EPS = 1e-5
RMS_EPS = 1e-5

COLS = (D_SSM,
        D_SSM,
        SSM_GROUPS * SSM_STATE,
        SSM_GROUPS * SSM_STATE,
        SSM_HEADS,
        2 * D_CONF,
        D_CONF)
D_IN_PROJ = sum(COLS)
SPLITS = tuple(int(s) for s in np.cumsum(COLS)[:-1])

DEEPNORM_ALPHA = (2.0 * DEPTH) ** 0.25
DEEPNORM_BETA = (8.0 * DEPTH) ** -0.25

kernel_name = "hybrid_ssd_conformer_deepnorm_block"


def layer_norm(x, g, b):
    xf = x.astype(jnp.float32)
    mu = jnp.mean(xf, axis=-1, keepdims=True)
    var = jnp.mean(jnp.square(xf - mu), axis=-1, keepdims=True)
    return ((xf - mu) * lax.rsqrt(var + LN_EPS)).astype(x.dtype) * g + b


def causal_depthwise_conv(x, w, b):
    k = w.shape[0]
    xp = jnp.pad(x, ((0, 0), (k - 1, 0), (0, 0)))
    y = lax.conv_general_dilated(xp, w[:, None, :], window_strides=(1,), padding='VALID',
                                 dimension_numbers=('NWC', 'WIO', 'NWC'),
                                 feature_group_count=x.shape[-1])
    return y + b


def ssd_chunked(xh, dt, a, bm, cm):
    bsz, l, g, j, p = xh.shape
    n = bm.shape[-1]
    c = l // CHUNK
    xdt = (xh * dt[..., None]).reshape(bsz, c, CHUNK, g, j, p)
    adt = jnp.moveaxis((dt * a).reshape(bsz, c, CHUNK, g, j), 2, -1)
    a_cs = jnp.cumsum(adt, axis=-1)
    bc = bm.reshape(bsz, c, CHUNK, g, n)
    cc = cm.reshape(bsz, c, CHUNK, g, n)
    causal = jnp.tril(jnp.ones((CHUNK, CHUNK), dtype=bool))
    seg = a_cs[..., :, None] - a_cs[..., None, :]
    ldec = jnp.exp(jnp.where(causal, seg, -jnp.inf))
    cb = jnp.einsum('bclgn,bcsgn->bcgls', cc, bc)
    y_diag = jnp.einsum('bcgjls,bcsgjp->bclgjp', cb[:, :, :, None] * ldec, xdt)
    decay_states = jnp.exp(a_cs[..., -1:] - a_cs)
    states = jnp.einsum('bclgn,bcgjl,bclgjp->bcgjpn', bc, decay_states, xdt)
    chunk_decay = jnp.exp(a_cs[..., -1])

    def step(h, inp):
        s, d = inp
        return h * d[..., None, None] + s, h

    h0 = jnp.zeros((bsz, g, j, p, n), dtype=xdt.dtype)
    _, prev = lax.scan(step, h0, (jnp.moveaxis(states, 1, 0), jnp.moveaxis(chunk_decay, 1, 0)))
    prev = jnp.moveaxis(prev, 0, 1)
    y_off = jnp.einsum('bclgn,bcgjpn,bcgjl->bclgjp', cc, prev, jnp.exp(a_cs))
    return (y_diag + y_off).reshape(bsz, l, g, j, p)


def setup_inputs(seed: int = 0) -> dict:
    key = jax.random.key(seed)
    ks = jax.random.split(key, 26)
    nrm = jax.random.normal
    f32 = jnp.float32
    x = nrm(ks[0], (BATCH, SEQ, D_MODEL), f32)
    p = nrm(ks[1], (DEPTH, BATCH, SEQ, D_PLE), f32)
    ln_emb_g = 1.0 + 0.02 * nrm(ks[2], (D_MODEL,), f32)
    ln_emb_b = 0.02 * nrm(ks[3], (D_MODEL,), f32)
    w_in = nrm(ks[4], (DEPTH, D_MODEL, D_IN_PROJ), f32) * D_MODEL ** -0.5
    d_xbc = D_SSM + 2 * SSM_GROUPS * SSM_STATE
    ssm_conv_w = nrm(ks[5], (DEPTH, SSM_CONV_K, d_xbc), f32) * SSM_CONV_K ** -0.5
    ssm_conv_b = 0.02 * nrm(ks[6], (DEPTH, d_xbc), f32)
    dt0 = jnp.exp(jax.random.uniform(ks[7], (DEPTH, SSM_HEADS), f32, np.log(1e-3), np.log(1e-1)))
    dt_bias = dt0 + jnp.log(-jnp.expm1(-dt0))
    a_log = jnp.log(jax.random.uniform(ks[8], (DEPTH, SSM_HEADS), f32, 1.0, 16.0))
    d_skip = 1.0 + 0.1 * nrm(ks[9], (DEPTH, SSM_HEADS), f32)
    ssm_norm_g = 1.0 + 0.02 * nrm(ks[10], (DEPTH, D_SSM), f32)
    b_glu = 0.02 * nrm(ks[11], (DEPTH, 2 * D_CONF), f32)
    conf_conv_w = nrm(ks[12], (DEPTH, CONF_K, D_CONF), f32) * CONF_K ** -0.5
    conf_conv_b = 0.02 * nrm(ks[13], (DEPTH, D_CONF), f32)
    conf_ln_g = 1.0 + 0.02 * nrm(ks[14], (DEPTH, D_CONF), f32)
    conf_ln_b = 0.02 * nrm(ks[15], (DEPTH, D_CONF), f32)
    w_out = nrm(ks[16], (DEPTH, D_MIX, D_MODEL), f32) * (DEEPNORM_BETA * (2.0 / (D_MIX + D_MODEL)) ** 0.5)
    b_out = 0.02 * nrm(ks[17], (DEPTH, D_MODEL), f32)
    ln1_g = 1.0 + 0.02 * nrm(ks[18], (DEPTH, D_MODEL), f32)
    ln1_b = 0.02 * nrm(ks[19], (DEPTH, D_MODEL), f32)
    w_ple_gate = nrm(ks[20], (DEPTH, D_MODEL, D_MODEL), f32) * D_MODEL ** -0.5
    w_ple_proj = nrm(ks[21], (DEPTH, D_PLE, D_MODEL), f32) * (DEEPNORM_BETA * (2.0 / (D_PLE + D_MODEL)) ** 0.5)
    ln2_g = 1.0 + 0.02 * nrm(ks[22], (DEPTH, D_MODEL), f32)
    ln2_b = 0.02 * nrm(ks[23], (DEPTH, D_MODEL), f32)
    return {"x": x, "p": p, "ln_emb_g": ln_emb_g, "ln_emb_b": ln_emb_b, "w_in": w_in,
            "ssm_conv_w": ssm_conv_w, "ssm_conv_b": ssm_conv_b, "dt_bias": dt_bias,
            "a_log": a_log, "d_skip": d_skip, "ssm_norm_g": ssm_norm_g, "b_glu": b_glu,
            "conf_conv_w": conf_conv_w, "conf_conv_b": conf_conv_b, "conf_ln_g": conf_ln_g,
            "conf_ln_b": conf_ln_b, "w_out": w_out, "b_out": b_out, "ln1_g": ln1_g,
            "ln1_b": ln1_b, "w_ple_gate": w_ple_gate, "w_ple_proj": w_ple_proj,
            "ln2_g": ln2_g, "ln2_b": ln2_b}


def reference(x, p, ln_emb_g, ln_emb_b, w_in, ssm_conv_w, ssm_conv_b, dt_bias, a_log,
              d_skip, ssm_norm_g, b_glu, conf_conv_w, conf_conv_b, conf_ln_g, conf_ln_b,
              w_out, b_out, ln1_g, ln1_b, w_ple_gate, w_ple_proj, ln2_g, ln2_b):
    bsz, l, _ = x.shape
    f32 = jnp.float32
    gn = SSM_GROUPS * SSM_STATE
    h = layer_norm(x, ln_emb_g, ln_emb_b)
    for i in range(DEPTH):
        proj = jnp.einsum('bld,de->ble', h, w_in[i])
        xs, z, bm, cm, dt_raw, glu, cgate = jnp.split(proj, SPLITS, axis=-1)

        xbc = jnp.concatenate([xs, bm, cm], axis=-1)
        xbc = jax.nn.silu(causal_depthwise_conv(xbc, ssm_conv_w[i], ssm_conv_b[i]))
        xs_c = xbc[..., :D_SSM]
        bm_c = xbc[..., D_SSM:D_SSM + gn].reshape(bsz, l, SSM_GROUPS, SSM_STATE).astype(f32)
        cm_c = xbc[..., D_SSM + gn:].reshape(bsz, l, SSM_GROUPS, SSM_STATE).astype(f32)
        xh = xs_c.reshape(bsz, l, SSM_GROUPS, SSM_HPG, SSM_HEAD_DIM).astype(f32)
        dt = jax.nn.softplus((dt_raw + dt_bias[i]).astype(f32)).reshape(bsz, l, SSM_GROUPS, SSM_HPG)
        a = -jnp.exp(a_log[i].astype(f32)).reshape(SSM_GROUPS, SSM_HPG)
        y = ssd_chunked(xh, dt, a, bm_c, cm_c)
        y = y + d_skip[i].astype(f32).reshape(SSM_GROUPS, SSM_HPG)[:, :, None] * xh
        yz = y.reshape(bsz, l, SSM_GROUPS, D_SSM // SSM_GROUPS) * \
            jax.nn.silu(z.astype(f32)).reshape(bsz, l, SSM_GROUPS, D_SSM // SSM_GROUPS)
        yz = yz * lax.rsqrt(jnp.mean(jnp.square(yz), axis=-1, keepdims=True) + RMS_EPS)
        y_ssm = yz.reshape(bsz, l, D_SSM).astype(x.dtype) * ssm_norm_g[i]

        glu = glu + b_glu[i]
        u = glu[..., :D_CONF] * jax.nn.sigmoid(glu[..., D_CONF:])
        u = causal_depthwise_conv(u, conf_conv_w[i], conf_conv_b[i])
        u = jax.nn.silu(layer_norm(u, conf_ln_g[i], conf_ln_b[i]))
        y_conf = u * jax.nn.silu(cgate)

        mix = jnp.concatenate([y_ssm, y_conf], axis=-1)
        out = jnp.einsum('ble,ed->bld', mix, w_out[i]) + b_out[i]
        h = layer_norm(DEEPNORM_ALPHA * h + out, ln1_g[i], ln1_b[i])

        gate = jax.nn.sigmoid(jnp.einsum('bld,de->ble', h, w_ple_gate[i]))
        ple = jnp.einsum('blq,qd->bld', p[i], w_ple_proj[i])
        h = layer_norm(DEEPNORM_ALPHA * h + gate * ple, ln2_g[i], ln2_b[i])
    return h
```

```python
import functools

import jax
import jax.numpy as jnp
import numpy as np
from jax import lax
from jax.experimental import pallas as pl
from jax.experimental.pallas import tpu as pltpu

F32 = jnp.float32
BF16 = jnp.bfloat16

D_MODEL = 1024
D_PLE = 256
D_SSM = 1024
D_CONF = 1024
HEAD_DIM = 64
N_HEADS = 16
N_GROUPS = 2
GROUP_W = D_SSM // N_GROUPS
D_STATE = 128
SSM_K = 4
CONF_K = 31
CHUNK = 128
D_XBC = D_SSM + 2 * N_GROUPS * D_STATE
LN_EPS = 1e-5
RMS_EPS = 1e-5
ALPHA = 2.0 ** 0.25

LANES = 128
SUBLANES = 8
VMEM_LIMIT = 56 * 1024 * 1024

C_XBC = 0
C_Z = C_XBC + D_XBC
C_GLU_A = C_Z + D_SSM
C_GLU_G = C_GLU_A + D_CONF
C_CG = C_GLU_G + D_CONF
C_DT = C_CG + D_CONF
W_COLS = C_DT + LANES

SSM_HALO = SUBLANES
CONF_HALO = 32

FRONT_T = 256
BACK_T = 512


def _layer_norm(x, g, b):
    mu = jnp.mean(x, axis=-1, keepdims=True)
    xc = x - mu
    var = jnp.mean(xc * xc, axis=-1, keepdims=True)
    return xc * lax.rsqrt(var + LN_EPS) * g + b


def _sigmoid(x):
    return 1.0 / (1.0 + jnp.exp(-x))


def _silu(x):
    return x * _sigmoid(x)


def _softplus(x):
    return jnp.maximum(x, 0.0) + jnp.log1p(jnp.exp(-jnp.abs(x)))


def _bdot(a, b):
    return jnp.dot(a, b, preferred_element_type=F32)


def _split_bf16(x, parts):
    out = []
    r = x
    for i in range(parts):
        p = r.astype(BF16)
        out.append(p)
        if i + 1 < parts:
            r = r - p.astype(F32)
    return out


def _front_kernel(x_ref, lng_ref, lnb_ref, w_ref, scw_ref, scb_ref, dtb_ref,
                  bga_ref, bgg_ref, ccw_ref, ccb_ref, clg_ref, clb_ref,
                  xbc_out, z_out, dt_out, yconf_out, sbuf, cbuf):
    t = pl.program_id(1)
    T = x_ref.shape[0]

    h = _layer_norm(x_ref[...], lng_ref[...], lnb_ref[...])
    hb = h.astype(BF16)

    @pl.when(t == 0)
    def _():
        sbuf[0:SSM_HALO, :] = jnp.zeros((SSM_HALO, D_XBC), F32)
        cbuf[0:CONF_HALO, :] = jnp.zeros((CONF_HALO, D_CONF), F32)

    @pl.when(t > 0)
    def _():
        sbuf[0:SSM_HALO, :] = sbuf[T:T + SSM_HALO, :]
        cbuf[0:CONF_HALO, :] = cbuf[T:T + CONF_HALO, :]

    sbuf[SSM_HALO:SSM_HALO + T, :] = _bdot(hb, w_ref[:, C_XBC:C_XBC + D_XBC])
    acc = scb_ref[...] + scw_ref[0:1, :] * sbuf[pl.ds(SSM_HALO - (SSM_K - 1), T), :]
    for k in range(1, SSM_K):
        acc = acc + scw_ref[k:k + 1, :] * sbuf[pl.ds(SSM_HALO - (SSM_K - 1) + k, T), :]
    xbc_out[...] = _silu(acc)

    z_out[...] = _bdot(hb, w_ref[:, C_Z:C_Z + D_SSM])
    dt_out[...] = _softplus(_bdot(hb, w_ref[:, C_DT:C_DT + LANES]) + dtb_ref[...])

    ga = _bdot(hb, w_ref[:, C_GLU_A:C_GLU_A + D_CONF]) + bga_ref[...]
    gg = _bdot(hb, w_ref[:, C_GLU_G:C_GLU_G + D_CONF]) + bgg_ref[...]
    cbuf[CONF_HALO:CONF_HALO + T, :] = ga * _sigmoid(gg)
    base = CONF_HALO - (CONF_K - 1)
    acc = ccb_ref[...] + ccw_ref[0:1, :] * cbuf[pl.ds(base, T), :]
    for k in range(1, CONF_K):
        acc = acc + ccw_ref[k:k + 1, :] * cbuf[pl.ds(base + k, T), :]
    u = _silu(_layer_norm(acc, clg_ref[...], clb_ref[...]))
    cg = _bdot(hb, w_ref[:, C_CG:C_CG + D_CONF])
    yconf_out[...] = (u * _silu(cg)).astype(BF16)


def _front(x, ln_g, ln_b, w_all, scw, scb, dtb, bga, bgg, ccw, ccb, clg, clb):
    B, L, _ = x.shape
    T = FRONT_T
    row = lambda b, t: (b, t, 0)
    const = lambda b, t: (0, 0)

    def cspec(shape):
        return pl.BlockSpec(shape, const)

    return pl.pallas_call(
        _front_kernel,
        grid=(B, L // T),
        in_specs=[
            pl.BlockSpec((None, T, D_MODEL), row),
            cspec((1, D_MODEL)), cspec((1, D_MODEL)),
            cspec((D_MODEL, W_COLS)),
            cspec((SSM_K, D_XBC)), cspec((1, D_XBC)), cspec((1, LANES)),
            cspec((1, D_CONF)), cspec((1, D_CONF)),
            cspec((CONF_K, D_CONF)), cspec((1, D_CONF)),
            cspec((1, D_CONF)), cspec((1, D_CONF)),
        ],
        out_specs=[
            pl.BlockSpec((None, T, D_XBC), row),
            pl.BlockSpec((None, T, D_SSM), row),
            pl.BlockSpec((None, T, LANES), row),
            pl.BlockSpec((None, T, D_CONF), row),
        ],
        out_shape=[
            jax.ShapeDtypeStruct((B, L, D_XBC), F32),
            jax.ShapeDtypeStruct((B, L, D_SSM), F32),
            jax.ShapeDtypeStruct((B, L, LANES), F32),
            jax.ShapeDtypeStruct((B, L, D_CONF), BF16),
        ],
        scratch_shapes=[
            pltpu.VMEM((SSM_HALO + T, D_XBC), F32),
            pltpu.VMEM((CONF_HALO + T, D_CONF), F32),
        ],
        compiler_params=pltpu.CompilerParams(
            dimension_semantics=("arbitrary", "arbitrary"),
            vmem_limit_bytes=VMEM_LIMIT),
        name="front",
    )(x, ln_g, ln_b, w_all, scw, scb, dtb, bga, bgg, ccw, ccb, clg, clb)


def _ssd_kernel(xbc_ref, dt_ref, z_ref, alog_ref, dskip_ref, g_ref,
                expand_ref, tril_ref, y_out, state_ref):
    c = pl.program_id(1)

    @pl.when(c == 0)
    def _():
        state_ref[...] = jnp.zeros(state_ref.shape, F32)

    expand = expand_ref[...]
    tril = tril_ref[...]
    dt = dt_ref[...]
    a_small = -jnp.exp(alog_ref[...])
    lane = lax.broadcasted_iota(jnp.int32, (1, LANES), 1)
    adt = jnp.where(lane < N_HEADS, dt * a_small, 0.0)

    a_cs = None
    for piece in _split_bf16(adt, 3):
        d = _bdot(tril, piece)
        a_cs = d if a_cs is None else a_cs + d
    a_cs_t = a_cs.T

    def expand_heads(v, parts):
        out = None
        for piece in _split_bf16(v, parts):
            d = _bdot(piece, expand)
            out = d if out is None else out + d
        return out

    a_cs_x = expand_heads(a_cs, 3)
    dt_x = expand_heads(dt, 2)
    a_last_x = a_cs_x[CHUNK - 1:CHUNK, :]

    xs = xbc_ref[:, 0:D_SSM]
    xdt = xs * dt_x
    xdt_b = xdt.astype(BF16)
    xdt_decay_b = (xdt * jnp.exp(a_last_x - a_cs_x)).astype(BF16)
    exp_a_cs_x = jnp.exp(a_cs_x)

    ri = lax.broadcasted_iota(jnp.int32, (CHUNK, CHUNK), 0)
    ci = lax.broadcasted_iota(jnp.int32, (CHUNK, CHUNK), 1)
    causal = ri >= ci
    lane_lo = lax.broadcasted_iota(jnp.int32, (CHUNK, LANES), 1) < HEAD_DIM

    y_groups = []
    for g in range(N_GROUPS):
        glo = g * GROUP_W
        bm = xbc_ref[:, D_SSM + g * D_STATE:D_SSM + (g + 1) * D_STATE]
        cm = xbc_ref[:, D_SSM + N_GROUPS * D_STATE + g * D_STATE:
                     D_SSM + N_GROUPS * D_STATE + (g + 1) * D_STATE]
        bm_b = bm.astype(BF16)
        cm_b = cm.astype(BF16)
        cb = lax.dot_general(cm_b, bm_b, (((1,), (1,)), ((), ())),
                             preferred_element_type=F32)

        y_pairs = []
        for q in range(GROUP_W // LANES):
            lo = glo + q * LANES
            ms = []
            for e in range(2):
                j = (lo // HEAD_DIM) + e
                col = jnp.broadcast_to(a_cs[:, j:j + 1], (CHUNK, CHUNK))
                rowv = jnp.broadcast_to(a_cs_t[j:j + 1, :], (CHUNK, CHUNK))
                ldec = jnp.exp(jnp.where(causal, col - rowv, -jnp.inf))
                ms.append((cb * ldec).astype(BF16))
            m_pair = jnp.concatenate(ms, axis=1)
            xp = xdt_b[:, lo:lo + LANES]
            zero = jnp.zeros_like(xp)
            rhs = jnp.concatenate([jnp.where(lane_lo, xp, zero),
                                   jnp.where(lane_lo, zero, xp)], axis=0)
            y_pairs.append(_bdot(m_pair, rhs))
        y_diag = jnp.concatenate(y_pairs, axis=1)

        prev = state_ref[:, glo:glo + GROUP_W]
        y_off = _bdot(cm_b, prev.astype(BF16)) * exp_a_cs_x[:, glo:glo + GROUP_W]

        st = lax.dot_general(bm_b, xdt_decay_b[:, glo:glo + GROUP_W],
                             (((0,), (0,)), ((), ())), preferred_element_type=F32)
        state_ref[:, glo:glo + GROUP_W] = prev * jnp.exp(a_last_x[:, glo:glo + GROUP_W]) + st

        y = y_diag + y_off + dskip_ref[:, glo:glo + GROUP_W] * xs[:, glo:glo + GROUP_W]
        yz = y * _silu(z_ref[:, glo:glo + GROUP_W])
        ms2 = jnp.mean(yz * yz, axis=-1, keepdims=True)
        y_groups.append(yz * lax.rsqrt(ms2 + RMS_EPS) * g_ref[:, glo:glo + GROUP_W])
    y_out[...] = jnp.concatenate(y_groups, axis=1).astype(BF16)


def _ssd(xbc, dt, z, alog, dskip, norm_g, expand, tril):
    B, L, _ = xbc.shape
    row = lambda b, c: (b, c, 0)
    const = lambda b, c: (0, 0)
    return pl.pallas_call(
        _ssd_kernel,
        grid=(B, L // CHUNK),
        in_specs=[
            pl.BlockSpec((None, CHUNK, D_XBC), row),
            pl.BlockSpec((None, CHUNK, LANES), row),
            pl.BlockSpec((None, CHUNK, D_SSM), row),
            pl.BlockSpec((1, LANES), const),
            pl.BlockSpec((1, D_SSM), const),
            pl.BlockSpec((1, D_SSM), const),
            pl.BlockSpec((LANES, D_SSM), const),
            pl.BlockSpec((CHUNK, CHUNK), const),
        ],
        out_specs=pl.BlockSpec((None, CHUNK, D_SSM), row),
        out_shape=jax.ShapeDtypeStruct((B, L, D_SSM), BF16),
        scratch_shapes=[pltpu.VMEM((D_STATE, D_SSM), F32)],
        compiler_params=pltpu.CompilerParams(
            dimension_semantics=("arbitrary", "arbitrary"),
            vmem_limit_bytes=VMEM_LIMIT),
        name="ssd",
    )(xbc, dt, z, alog, dskip, norm_g, expand, tril)


def _back_kernel(x_ref, p_ref, yssm_ref, yconf_ref, lng_ref, lnb_ref,
                 wo_s_ref, wo_c_ref, bo_ref, l1g_ref, l1b_ref,
                 wg_ref, wp_ref, l2g_ref, l2b_ref, out_ref):
    h = _layer_norm(x_ref[...], lng_ref[...], lnb_ref[...])
    out = (_bdot(yssm_ref[...], wo_s_ref[...]) + _bdot(yconf_ref[...], wo_c_ref[...])
           + bo_ref[...])
    h1 = _layer_norm(ALPHA * h + out, l1g_ref[...], l1b_ref[...])
    gate = _sigmoid(_bdot(h1.astype(BF16), wg_ref[...]))
    ple = _bdot(p_ref[...].astype(BF16), wp_ref[...])
    out_ref[...] = _layer_norm(ALPHA * h1 + gate * ple, l2g_ref[...], l2b_ref[...])


def _back(x2, p2, yssm, yconf, ln_g, ln_b, wo_s, wo_c, bo, l1g, l1b, wg, wp, l2g, l2b):
    M = x2.shape[0]
    T = BACK_T
    row = lambda i: (i, 0)
    const = lambda i: (0, 0)

    def cspec(shape):
        return pl.BlockSpec(shape, const)

    return pl.pallas_call(
        _back_kernel,
        grid=(M // T,),
        in_specs=[
            pl.BlockSpec((T, D_MODEL), row),
            pl.BlockSpec((T, D_PLE), row),
            pl.BlockSpec((T, D_SSM), row),
            pl.BlockSpec((T, D_CONF), row),
            cspec((1, D_MODEL)), cspec((1, D_MODEL)),
            cspec((D_SSM, D_MODEL)), cspec((D_CONF, D_MODEL)), cspec((1, D_MODEL)),
            cspec((1, D_MODEL)), cspec((1, D_MODEL)),
            cspec((D_MODEL, D_MODEL)), cspec((D_PLE, D_MODEL)),
            cspec((1, D_MODEL)), cspec((1, D_MODEL)),
        ],
        out_specs=pl.BlockSpec((T, D_MODEL), row),
        out_shape=jax.ShapeDtypeStruct((M, D_MODEL), F32),
        compiler_params=pltpu.CompilerParams(
            dimension_semantics=("arbitrary",),
            vmem_limit_bytes=VMEM_LIMIT),
        name="back",
    )(x2, p2, yssm, yconf, ln_g, ln_b, wo_s, wo_c, bo, l1g, l1b, wg, wp, l2g, l2b)


def _row(v):
    return v.reshape(1, -1).astype(F32)


def kernel(x, p, ln_emb_g, ln_emb_b, w_in, ssm_conv_w, ssm_conv_b, dt_bias, a_log, d_skip,
           ssm_norm_g, b_glu, conf_conv_w, conf_conv_b, conf_ln_g, conf_ln_b, w_out, b_out,
           ln1_g, ln1_b, w_ple_gate, w_ple_proj, ln2_g, ln2_b):
    B, L, _ = x.shape
    gn = N_GROUPS * D_STATE
    o_z = D_SSM
    o_b = o_z + D_SSM
    o_c = o_b + gn
    o_dt = o_c + gn
    o_ga = o_dt + N_HEADS
    o_gg = o_ga + D_CONF
    o_cg = o_gg + D_CONF
    w = w_in[0]
    w_all = jnp.concatenate([
        w[:, 0:D_SSM], w[:, o_b:o_c], w[:, o_c:o_dt],
        w[:, o_z:o_b],
        w[:, o_ga:o_gg], w[:, o_gg:o_cg], w[:, o_cg:o_cg + D_CONF],
        w[:, o_dt:o_ga], jnp.zeros((D_MODEL, LANES - N_HEADS), F32),
    ], axis=1).astype(BF16)

    pad16 = lambda v: jnp.concatenate([v.astype(F32), jnp.zeros((LANES - N_HEADS,), F32)]).reshape(1, LANES)

    xbc, z, dt, yconf = _front(
        x, _row(ln_emb_g), _row(ln_emb_b), w_all,
        ssm_conv_w[0].astype(F32), _row(ssm_conv_b[0]), pad16(dt_bias[0]),
        _row(b_glu[0, :D_CONF]), _row(b_glu[0, D_CONF:]),
        conf_conv_w[0].astype(F32), _row(conf_conv_b[0]),
        _row(conf_ln_g[0]), _row(conf_ln_b[0]))

    head_of_lane = np.arange(D_SSM) // HEAD_DIM
    expand = jnp.asarray(np.arange(LANES)[:, None] == head_of_lane[None, :], dtype=BF16)
    tril = jnp.asarray(np.tril(np.ones((CHUNK, CHUNK), np.float32)), dtype=BF16)
    yssm = _ssd(xbc, dt, z, pad16(a_log[0]),
                _row(jnp.repeat(d_skip[0], HEAD_DIM)), _row(ssm_norm_g[0]), expand, tril)

    wo = w_out[0].astype(BF16)
    out = _back(
        x.reshape(B * L, D_MODEL), p[0].reshape(B * L, D_PLE),
        yssm.reshape(B * L, D_SSM), yconf.reshape(B * L, D_CONF),
        _row(ln_emb_g), _row(ln_emb_b), wo[:D_SSM], wo[D_SSM:], _row(b_out[0]),
        _row(ln1_g[0]), _row(ln1_b[0]),
        w_ple_gate[0].astype(BF16), w_ple_proj[0].astype(BF16),
        _row(ln2_g[0]), _row(ln2_b[0]))
    return out.reshape(B, L, D_MODEL)
```

```python
import functools

import jax
import jax.numpy as jnp
import numpy as np
from jax import lax
from jax.experimental import pallas as pl
from jax.experimental.pallas import tpu as pltpu

F32 = jnp.float32
BF16 = jnp.bfloat16

D_MODEL = 1024
D_PLE = 256
D_SSM = 1024
D_CONF = 1024
HEAD_DIM = 64
N_HEADS = 16
N_GROUPS = 2
GROUP_W = D_SSM // N_GROUPS
D_STATE = 128
SSM_K = 4
CONF_K = 31
CHUNK = 128
D_XBC = D_SSM + 2 * N_GROUPS * D_STATE
LN_EPS = 1e-5
RMS_EPS = 1e-5
ALPHA = 2.0 ** 0.25

LANES = 128
SUBLANES = 8
VMEM_LIMIT = 56 * 1024 * 1024

C_XBC = 0
C_Z = C_XBC + D_XBC
C_GLU_A = C_Z + D_SSM
C_GLU_G = C_GLU_A + D_CONF
C_CG = C_GLU_G + D_CONF
C_DT = C_CG + D_CONF
W_COLS = C_DT + LANES

SSM_HALO = SUBLANES
CONF_HALO = 32
CONF_SHIFT_ROWS = CONF_HALO - SUBLANES
SSM_ROWS = 16
CONF_ROWS = 32
SSD_CHUNKS_PER_STEP = 4

FRONT_T = 256
BACK_T = 512


def _layer_norm(x, g, b):
    mu = jnp.mean(x, axis=-1, keepdims=True)
    xc = x - mu
    var = jnp.mean(xc * xc, axis=-1, keepdims=True)
    return xc * lax.rsqrt(var + LN_EPS) * g + b


def _sigmoid(x):
    return 0.5 * jnp.tanh(0.5 * x) + 0.5


def _silu(x):
    hx = 0.5 * x
    return hx * jnp.tanh(hx) + hx


def _softplus(x):
    return jnp.maximum(x, 0.0) + jnp.log1p(jnp.exp(-jnp.abs(x)))


def _bdot(a, b):
    return jnp.dot(a, b, preferred_element_type=F32)


def _split_bf16(x, parts):
    out = []
    r = x
    for i in range(parts):
        p = r.astype(BF16)
        out.append(p)
        if i + 1 < parts:
            r = r - p.astype(F32)
    return out


def _front_kernel(x_ref, lng_ref, lnb_ref, w_ref, scw_ref, scb_ref, dtb_ref,
                  bga_ref, bgg_ref, ccw_ref, ccb_ref, clg_ref, clb_ref,
                  xbc_out, z_out, dt_out, yconf_out, sbuf, cbuf, pbuf, cg_ref):
    t = pl.program_id(1)
    T = x_ref.shape[0]

    h = _layer_norm(x_ref[...], lng_ref[...], lnb_ref[...])
    hb = h.astype(BF16)

    @pl.when(t == 0)
    def _():
        sbuf[0:SSM_HALO, :] = jnp.zeros((SSM_HALO, D_XBC), F32)
        cbuf[0:CONF_HALO, :] = jnp.zeros((CONF_HALO, D_CONF), F32)

    @pl.when(t > 0)
    def _():
        sbuf[0:SSM_HALO, :] = sbuf[T:T + SSM_HALO, :]
        cbuf[0:CONF_HALO, :] = cbuf[T:T + CONF_HALO, :]

    sbuf[SSM_HALO:SSM_HALO + T, :] = _bdot(hb, w_ref[:, C_XBC:C_XBC + D_XBC])
    base = SSM_HALO - (SSM_K - 1)
    for r0 in range(0, T, SUBLANES):
        acc = scb_ref[...]
        for k in range(SSM_K):
            acc = acc + scw_ref[k] * sbuf[base + k + r0:base + k + r0 + SUBLANES, :]
        xbc_out[r0:r0 + SUBLANES, :] = _silu(acc)

    z_out[...] = _bdot(hb, w_ref[:, C_Z:C_Z + D_SSM])
    dt_out[...] = _softplus(_bdot(hb, w_ref[:, C_DT:C_DT + LANES]) + dtb_ref[...])

    ga = _bdot(hb, w_ref[:, C_GLU_A:C_GLU_A + D_CONF]) + bga_ref[...]
    gg = _bdot(hb, w_ref[:, C_GLU_G:C_GLU_G + D_CONF]) + bgg_ref[...]
    cbuf[CONF_HALO:CONF_HALO + T, :] = ga * _sigmoid(gg)
    cg_ref[...] = _silu(_bdot(hb, w_ref[:, C_CG:C_CG + D_CONF]))
    for r in range(1, SUBLANES):
        pbuf[r - 1] = cbuf[r:r + T + CONF_SHIFT_ROWS, :]
    base = CONF_HALO - (CONF_K - 1)
    tiles = CONF_ROWS // SUBLANES
    for r0 in range(0, T, CONF_ROWS):
        accs = [ccb_ref[...]] * tiles
        for k in range(CONF_K):
            a, r = divmod(base + k, SUBLANES)
            wk = ccw_ref[k]
            for i in range(tiles):
                lo = r0 + (a + i) * SUBLANES
                if r == 0:
                    tap = cbuf[lo:lo + SUBLANES, :]
                else:
                    tap = pbuf[r - 1, lo:lo + SUBLANES, :]
                accs[i] = accs[i] + wk * tap
        acc = jnp.concatenate(accs, axis=0)
        u = _silu(_layer_norm(acc, clg_ref[...], clb_ref[...]))
        yconf_out[r0:r0 + CONF_ROWS, :] = (u * cg_ref[r0:r0 + CONF_ROWS, :]).astype(BF16)


def _front(x, ln_g, ln_b, w_all, scw, scb, dtb, bga, bgg, ccw, ccb, clg, clb):
    B, L, _ = x.shape
    T = FRONT_T
    row = lambda b, t: (b, t, 0)
    const = lambda b, t: (0, 0)

    def cspec(shape):
        return pl.BlockSpec(shape, const)

    return pl.pallas_call(
        _front_kernel,
        grid=(B, L // T),
        in_specs=[
            pl.BlockSpec((None, T, D_MODEL), row),
            cspec((1, D_MODEL)), cspec((1, D_MODEL)),
            cspec((D_MODEL, W_COLS)),
            pl.BlockSpec((SSM_K, SUBLANES, D_XBC), lambda b, t: (0, 0, 0)),
            cspec((SUBLANES, D_XBC)), cspec((1, LANES)),
            cspec((1, D_CONF)), cspec((1, D_CONF)),
            pl.BlockSpec((CONF_K, SUBLANES, D_CONF), lambda b, t: (0, 0, 0)),
            cspec((SUBLANES, D_CONF)),
            cspec((1, D_CONF)), cspec((1, D_CONF)),
        ],
        out_specs=[
            pl.BlockSpec((None, T, D_XBC), row),
            pl.BlockSpec((None, T, D_SSM), row),
            pl.BlockSpec((None, T, LANES), row),
            pl.BlockSpec((None, T, D_CONF), row),
        ],
        out_shape=[
            jax.ShapeDtypeStruct((B, L, D_XBC), F32),
            jax.ShapeDtypeStruct((B, L, D_SSM), F32),
            jax.ShapeDtypeStruct((B, L, LANES), F32),
            jax.ShapeDtypeStruct((B, L, D_CONF), BF16),
        ],
        scratch_shapes=[
            pltpu.VMEM((SSM_HALO + T, D_XBC), F32),
            pltpu.VMEM((CONF_HALO + T, D_CONF), F32),
            pltpu.VMEM((SUBLANES - 1, T + CONF_SHIFT_ROWS, D_CONF), F32),
            pltpu.VMEM((T, D_CONF), F32),
        ],
        compiler_params=pltpu.CompilerParams(
            dimension_semantics=("arbitrary", "arbitrary"),
            vmem_limit_bytes=VMEM_LIMIT),
        name="front",
    )(x, ln_g, ln_b, w_all, scw, scb, dtb, bga, bgg, ccw, ccb, clg, clb)


def _ssd_kernel(xbc_ref, dt_ref, z_ref, alog_ref, dskip_ref, g_ref,
                expand_ref, tril_ref, y_out, state_ref):
    @pl.when(pl.program_id(1) == 0)
    def _():
        state_ref[...] = jnp.zeros(state_ref.shape, F32)

    for ci in range(SSD_CHUNKS_PER_STEP):
        rows = pl.ds(ci * CHUNK, CHUNK)
        _ssd_chunk(xbc_ref.at[rows], dt_ref.at[rows], z_ref.at[rows], alog_ref, dskip_ref,
                   g_ref, expand_ref, tril_ref, y_out.at[rows], state_ref)


def _ssd_chunk(xbc_ref, dt_ref, z_ref, alog_ref, dskip_ref, g_ref,
               expand_ref, tril_ref, y_out, state_ref):
    expand = expand_ref[...]
    tril = tril_ref[...]
    dt = dt_ref[...]
    a_small = -jnp.exp(alog_ref[...])
    lane = lax.broadcasted_iota(jnp.int32, (1, LANES), 1)
    adt = jnp.where(lane < N_HEADS, dt * a_small, 0.0)

    a_cs = None
    for piece in _split_bf16(adt, 3):
        d = _bdot(tril, piece)
        a_cs = d if a_cs is None else a_cs + d
    a_cs_t = a_cs.T

    def expand_heads(v, parts):
        out = None
        for piece in _split_bf16(v, parts):
            d = _bdot(piece, expand)
            out = d if out is None else out + d
        return out

    a_cs_x = expand_heads(a_cs, 3)
    dt_x = expand_heads(dt, 2)
    a_last_x = a_cs_x[CHUNK - 1:CHUNK, :]

    xs = xbc_ref[:, 0:D_SSM]
    xdt = xs * dt_x
    xdt_b = xdt.astype(BF16)
    xdt_decay_b = (xdt * jnp.exp(a_last_x - a_cs_x)).astype(BF16)
    exp_a_cs_x = jnp.exp(a_cs_x)

    ri = lax.broadcasted_iota(jnp.int32, (CHUNK, CHUNK), 0)
    ci = lax.broadcasted_iota(jnp.int32, (CHUNK, CHUNK), 1)
    causal = ri >= ci
    lane_lo = lax.broadcasted_iota(jnp.int32, (CHUNK, LANES), 1) < HEAD_DIM

    y_groups = []
    for g in range(N_GROUPS):
        glo = g * GROUP_W
        bm = xbc_ref[:, D_SSM + g * D_STATE:D_SSM + (g + 1) * D_STATE]
        cm = xbc_ref[:, D_SSM + N_GROUPS * D_STATE + g * D_STATE:
                     D_SSM + N_GROUPS * D_STATE + (g + 1) * D_STATE]
        bm_b = bm.astype(BF16)
        cm_b = cm.astype(BF16)
        cb = lax.dot_general(cm_b, bm_b, (((1,), (1,)), ((), ())),
                             preferred_element_type=F32)

        y_pairs = []
        for q in range(GROUP_W // LANES):
            lo = glo + q * LANES
            ms = []
            for e in range(2):
                j = (lo // HEAD_DIM) + e
                col = jnp.broadcast_to(a_cs[:, j:j + 1], (CHUNK, CHUNK))
                rowv = jnp.broadcast_to(a_cs_t[j:j + 1, :], (CHUNK, CHUNK))
                ldec = jnp.exp(jnp.where(causal, col - rowv, -jnp.inf))
                ms.append((cb * ldec).astype(BF16))
            m_pair = jnp.concatenate(ms, axis=1)
            xp = xdt_b[:, lo:lo + LANES]
            zero = jnp.zeros_like(xp)
            rhs = jnp.concatenate([jnp.where(lane_lo, xp, zero),
                                   jnp.where(lane_lo, zero, xp)], axis=0)
            y_pairs.append(_bdot(m_pair, rhs))
        y_diag = jnp.concatenate(y_pairs, axis=1)

        prev = state_ref[:, glo:glo + GROUP_W]
        y_off = _bdot(cm_b, prev.astype(BF16)) * exp_a_cs_x[:, glo:glo + GROUP_W]

        st = lax.dot_general(bm_b, xdt_decay_b[:, glo:glo + GROUP_W],
                             (((0,), (0,)), ((), ())), preferred_element_type=F32)
        state_ref[:, glo:glo + GROUP_W] = prev * jnp.exp(a_last_x[:, glo:glo + GROUP_W]) + st

        y = y_diag + y_off + dskip_ref[:, glo:glo + GROUP_W] * xs[:, glo:glo + GROUP_W]
        yz = y * _silu(z_ref[:, glo:glo + GROUP_W])
        ms2 = jnp.mean(yz * yz, axis=-1, keepdims=True)
        y_groups.append(yz * lax.rsqrt(ms2 + RMS_EPS) * g_ref[:, glo:glo + GROUP_W])
    y_out[...] = jnp.concatenate(y_groups, axis=1).astype(BF16)


def _ssd(xbc, dt, z, alog, dskip, norm_g, expand, tril):
    B, L, _ = xbc.shape
    row = lambda b, c: (b, c, 0)
    const = lambda b, c: (0, 0)
    T = SSD_CHUNKS_PER_STEP * CHUNK
    return pl.pallas_call(
        _ssd_kernel,
        grid=(B, L // T),
        in_specs=[
            pl.BlockSpec((None, T, D_XBC), row),
            pl.BlockSpec((None, T, LANES), row),
            pl.BlockSpec((None, T, D_SSM), row),
            pl.BlockSpec((1, LANES), const),
            pl.BlockSpec((1, D_SSM), const),
            pl.BlockSpec((1, D_SSM), const),
            pl.BlockSpec((LANES, D_SSM), const),
            pl.BlockSpec((CHUNK, CHUNK), const),
        ],
        out_specs=pl.BlockSpec((None, T, D_SSM), row),
        out_shape=jax.ShapeDtypeStruct((B, L, D_SSM), BF16),
        scratch_shapes=[pltpu.VMEM((D_STATE, D_SSM), F32)],
        compiler_params=pltpu.CompilerParams(
            dimension_semantics=("arbitrary", "arbitrary"),
            vmem_limit_bytes=VMEM_LIMIT),
        name="ssd",
    )(xbc, dt, z, alog, dskip, norm_g, expand, tril)


def _back_kernel(x_ref, p_ref, yssm_ref, yconf_ref, lng_ref, lnb_ref,
                 wo_s_ref, wo_c_ref, bo_ref, l1g_ref, l1b_ref,
                 wg_ref, wp_ref, l2g_ref, l2b_ref, out_ref):
    h = _layer_norm(x_ref[...], lng_ref[...], lnb_ref[...])
    out = (_bdot(yssm_ref[...], wo_s_ref[...]) + _bdot(yconf_ref[...], wo_c_ref[...])
           + bo_ref[...])
    h1 = _layer_norm(ALPHA * h + out, l1g_ref[...], l1b_ref[...])
    gate = _sigmoid(_bdot(h1.astype(BF16), wg_ref[...]))
    ple = _bdot(p_ref[...].astype(BF16), wp_ref[...])
    out_ref[...] = _layer_norm(ALPHA * h1 + gate * ple, l2g_ref[...], l2b_ref[...])


def _back(x2, p2, yssm, yconf, ln_g, ln_b, wo_s, wo_c, bo, l1g, l1b, wg, wp, l2g, l2b):
    M = x2.shape[0]
    T = BACK_T
    row = lambda i: (i, 0)
    const = lambda i: (0, 0)

    def cspec(shape):
        return pl.BlockSpec(shape, const)

    return pl.pallas_call(
        _back_kernel,
        grid=(M // T,),
        in_specs=[
            pl.BlockSpec((T, D_MODEL), row),
            pl.BlockSpec((T, D_PLE), row),
            pl.BlockSpec((T, D_SSM), row),
            pl.BlockSpec((T, D_CONF), row),
            cspec((1, D_MODEL)), cspec((1, D_MODEL)),
            cspec((D_SSM, D_MODEL)), cspec((D_CONF, D_MODEL)), cspec((1, D_MODEL)),
            cspec((1, D_MODEL)), cspec((1, D_MODEL)),
            cspec((D_MODEL, D_MODEL)), cspec((D_PLE, D_MODEL)),
            cspec((1, D_MODEL)), cspec((1, D_MODEL)),
        ],
        out_specs=pl.BlockSpec((T, D_MODEL), row),
        out_shape=jax.ShapeDtypeStruct((M, D_MODEL), F32),
        compiler_params=pltpu.CompilerParams(
            dimension_semantics=("arbitrary",),
            vmem_limit_bytes=VMEM_LIMIT),
        name="back",
    )(x2, p2, yssm, yconf, ln_g, ln_b, wo_s, wo_c, bo, l1g, l1b, wg, wp, l2g, l2b)


def _row(v):
    return v.reshape(1, -1).astype(F32)


def _tile_rows(v):
    v = v.astype(F32)
    return jnp.broadcast_to(v[..., None, :], v.shape[:-1] + (SUBLANES, v.shape[-1]))


def kernel(x, p, ln_emb_g, ln_emb_b, w_in, ssm_conv_w, ssm_conv_b, dt_bias, a_log, d_skip,
           ssm_norm_g, b_glu, conf_conv_w, conf_conv_b, conf_ln_g, conf_ln_b, w_out, b_out,
           ln1_g, ln1_b, w_ple_gate, w_ple_proj, ln2_g, ln2_b):
    B, L, _ = x.shape
    gn = N_GROUPS * D_STATE
    o_z = D_SSM
    o_b = o_z + D_SSM
    o_c = o_b + gn
    o_dt = o_c + gn
    o_ga = o_dt + N_HEADS
    o_gg = o_ga + D_CONF
    o_cg = o_gg + D_CONF
    w = w_in[0]
    w_all = jnp.concatenate([
        w[:, 0:D_SSM], w[:, o_b:o_c], w[:, o_c:o_dt],
        w[:, o_z:o_b],
        w[:, o_ga:o_gg], w[:, o_gg:o_cg], w[:, o_cg:o_cg + D_CONF],
        w[:, o_dt:o_ga], jnp.zeros((D_MODEL, LANES - N_HEADS), F32),
    ], axis=1).astype(BF16)

    pad16 = lambda v: jnp.concatenate([v.astype(F32), jnp.zeros((LANES - N_HEADS,), F32)]).reshape(1, LANES)

    xbc, z, dt, yconf = _front(
        x, _row(ln_emb_g), _row(ln_emb_b), w_all,
        _tile_rows(ssm_conv_w[0]), _tile_rows(ssm_conv_b[0]), pad16(dt_bias[0]),
        _row(b_glu[0, :D_CONF]), _row(b_glu[0, D_CONF:]),
        _tile_rows(conf_conv_w[0]), _tile_rows(conf_conv_b[0]),
        _row(conf_ln_g[0]), _row(conf_ln_b[0]))

    head_of_lane = np.arange(D_SSM) // HEAD_DIM
    expand = jnp.asarray(np.arange(LANES)[:, None] == head_of_lane[None, :], dtype=BF16)
    tril = jnp.asarray(np.tril(np.ones((CHUNK, CHUNK), np.float32)), dtype=BF16)
    yssm = _ssd(xbc, dt, z, pad16(a_log[0]),
                _row(jnp.repeat(d_skip[0], HEAD_DIM)), _row(ssm_norm_g[0]), expand, tril)

    wo = w_out[0].astype(BF16)
    out = _back(
        x.reshape(B * L, D_MODEL), p[0].reshape(B * L, D_PLE),
        yssm.reshape(B * L, D_SSM), yconf.reshape(B * L, D_CONF),
        _row(ln_emb_g), _row(ln_emb_b), wo[:D_SSM], wo[D_SSM:], _row(b_out[0]),
        _row(ln1_g[0]), _row(ln1_b[0]),
        w_ple_gate[0].astype(BF16), w_ple_proj[0].astype(BF16),
        _row(ln2_g[0]), _row(ln2_b[0]))
    return out.reshape(B, L, D_MODEL)
```

```python
import functools

import jax
import jax.numpy as jnp
import numpy as np
from jax import lax
from jax.experimental import pallas as pl
from jax.experimental.pallas import tpu as pltpu

F32 = jnp.float32
BF16 = jnp.bfloat16

D_MODEL = 1024
D_PLE = 256
D_SSM = 1024
D_CONF = 1024
HEAD_DIM = 64
N_HEADS = 16
N_GROUPS = 2
GROUP_W = D_SSM // N_GROUPS
D_STATE = 128
SSM_K = 4
CONF_K = 31
CHUNK = 128
D_XBC = D_SSM + 2 * N_GROUPS * D_STATE
LN_EPS = 1e-5
RMS_EPS = 1e-5
ALPHA = 2.0 ** 0.25

LANES = 128
SUBLANES = 8
VMEM_LIMIT = 56 * 1024 * 1024

C_XBC = 0
C_Z = C_XBC + D_XBC
C_GLU_A = C_Z + D_SSM
C_GLU_G = C_GLU_A + D_CONF
C_CG = C_GLU_G + D_CONF
C_DT = C_CG + D_CONF
W_COLS = C_DT + LANES

SSM_HALO = SUBLANES
CONF_HALO = 32
CONF_SHIFT_ROWS = CONF_HALO - SUBLANES
CONF_ROWS = 32
SSD_CHUNKS_PER_STEP = 4

FRONT_T = 256
D_IN_PROJ = 2 * D_SSM + 2 * N_GROUPS * D_STATE + N_HEADS + 3 * D_CONF
W_PREP_ROWS = 64
BACK_T = 1024
BACK_ROWS = 512


def _layer_norm(x, g, b):
    mu = jnp.mean(x, axis=-1, keepdims=True)
    xc = x - mu
    var = jnp.mean(xc * xc, axis=-1, keepdims=True)
    return xc * lax.rsqrt(var + LN_EPS) * g + b


def _sigmoid(x):
    return 0.5 * jnp.tanh(0.5 * x) + 0.5


def _silu(x):
    hx = 0.5 * x
    return hx * jnp.tanh(hx) + hx


def _softplus(x):
    return jnp.maximum(x, 0.0) + jnp.log1p(jnp.exp(-jnp.abs(x)))


def _bdot(a, b):
    return jnp.dot(a, b, preferred_element_type=F32)


def _split_bf16(x, parts):
    out = []
    r = x
    for i in range(parts):
        p = r.astype(BF16)
        out.append(p)
        if i + 1 < parts:
            r = r - p.astype(F32)
    return out


def _load_in_proj_weight(w_hbm, w_ref, stage, sem):
    gn = N_GROUPS * D_STATE
    o_b = 2 * D_SSM
    o_dt = o_b + 2 * gn
    n_chunks = D_MODEL // W_PREP_ROWS

    def chunk_copy(c):
        return pltpu.make_async_copy(w_hbm.at[0, pl.ds(c * W_PREP_ROWS, W_PREP_ROWS), :],
                                     stage.at[c % 2], sem.at[c % 2])

    chunk_copy(0).start()
    for c in range(n_chunks):
        if c + 1 < n_chunks:
            chunk_copy(c + 1).start()
        chunk_copy(c).wait()
        blk = stage.at[c % 2]
        rows = slice(c * W_PREP_ROWS, (c + 1) * W_PREP_ROWS)
        w_ref[rows, C_XBC:C_XBC + D_SSM] = blk[:, 0:D_SSM].astype(BF16)
        w_ref[rows, C_XBC + D_SSM:C_XBC + D_XBC] = blk[:, o_b:o_dt].astype(BF16)
        w_ref[rows, C_Z:C_Z + D_SSM] = blk[:, D_SSM:o_b].astype(BF16)
        tail = blk[:, o_dt:D_IN_PROJ]
        w_ref[rows, C_GLU_A:C_GLU_A + 3 * D_CONF] = tail[:, N_HEADS:].astype(BF16)
        lane = lax.broadcasted_iota(jnp.int32, (W_PREP_ROWS, LANES), 1)
        w_ref[rows, C_DT:C_DT + LANES] = jnp.where(lane < N_HEADS, tail[:, 0:LANES], 0.0).astype(BF16)


def _front_kernel(x_ref, lng_ref, lnb_ref, w_hbm, scw_ref, scb_ref, dtb_ref,
                  bga_ref, bgg_ref, ccw_ref, ccb_ref, clg_ref, clb_ref,
                  xbc_out, z_out, dt_out, yconf_out, sbuf, cbuf, pbuf, cg_ref, w_ref, stage, sem):
    t = pl.program_id(1)
    T = x_ref.shape[0]

    @pl.when(jnp.logical_and(pl.program_id(0) == 0, t == 0))
    def _():
        _load_in_proj_weight(w_hbm, w_ref, stage, sem)

    h = _layer_norm(x_ref[...], lng_ref[...], lnb_ref[...])
    hb = h.astype(BF16)

    @pl.when(t == 0)
    def _():
        sbuf[0:SSM_HALO, :] = jnp.zeros((SSM_HALO, D_XBC), F32)
        cbuf[0:CONF_HALO, :] = jnp.zeros((CONF_HALO, D_CONF), F32)

    @pl.when(t > 0)
    def _():
        sbuf[0:SSM_HALO, :] = sbuf[T:T + SSM_HALO, :]
        cbuf[0:CONF_HALO, :] = cbuf[T:T + CONF_HALO, :]

    ga = _bdot(hb, w_ref[:, C_GLU_A:C_GLU_A + D_CONF]) + bga_ref[...]
    gg = _bdot(hb, w_ref[:, C_GLU_G:C_GLU_G + D_CONF]) + bgg_ref[...]
    cbuf[CONF_HALO:CONF_HALO + T, :] = ga * _sigmoid(gg)
    for r in range(1, SUBLANES):
        pbuf[r - 1] = cbuf[r:r + T + CONF_SHIFT_ROWS, :]
    cg_ref[...] = _silu(_bdot(hb, w_ref[:, C_CG:C_CG + D_CONF]))
    sbuf[SSM_HALO:SSM_HALO + T, :] = _bdot(hb, w_ref[:, C_XBC:C_XBC + D_XBC])
    z_out[...] = _bdot(hb, w_ref[:, C_Z:C_Z + D_SSM])
    dt_out[...] = _softplus(_bdot(hb, w_ref[:, C_DT:C_DT + LANES]) + dtb_ref[...])

    base = CONF_HALO - (CONF_K - 1)
    tiles = CONF_ROWS // SUBLANES
    for r0 in range(0, T, CONF_ROWS):
        accs = [ccb_ref[...]] * tiles
        for k in range(CONF_K):
            a, r = divmod(base + k, SUBLANES)
            wk = ccw_ref[k]
            for i in range(tiles):
                lo = r0 + (a + i) * SUBLANES
                if r == 0:
                    tap = cbuf[lo:lo + SUBLANES, :]
                else:
                    tap = pbuf[r - 1, lo:lo + SUBLANES, :]
                accs[i] = accs[i] + wk * tap
        acc = jnp.concatenate(accs, axis=0)
        u = _silu(_layer_norm(acc, clg_ref[...], clb_ref[...]))
        yconf_out[r0:r0 + CONF_ROWS, :] = (u * cg_ref[r0:r0 + CONF_ROWS, :]).astype(BF16)

    base = SSM_HALO - (SSM_K - 1)
    for r0 in range(0, T, SUBLANES):
        acc = scb_ref[...]
        for k in range(SSM_K):
            acc = acc + scw_ref[k] * sbuf[base + k + r0:base + k + r0 + SUBLANES, :]
        xbc_out[r0:r0 + SUBLANES, :] = _silu(acc)


def _front(x, ln_g, ln_b, w_in, scw, scb, dtb, bga, bgg, ccw, ccb, clg, clb):
    B, L, _ = x.shape
    T = FRONT_T
    row = lambda b, t: (b, t, 0)
    const = lambda b, t: (0, 0)

    def cspec(shape):
        return pl.BlockSpec(shape, const)

    return pl.pallas_call(
        _front_kernel,
        grid=(B, L // T),
        in_specs=[
            pl.BlockSpec((None, T, D_MODEL), row),
            cspec((1, D_MODEL)), cspec((1, D_MODEL)),
            pl.BlockSpec(memory_space=pl.ANY),
            pl.BlockSpec((SSM_K, SUBLANES, D_XBC), lambda b, t: (0, 0, 0)),
            cspec((SUBLANES, D_XBC)), cspec((1, LANES)),
            cspec((1, D_CONF)), cspec((1, D_CONF)),
            pl.BlockSpec((CONF_K, SUBLANES, D_CONF), lambda b, t: (0, 0, 0)),
            cspec((SUBLANES, D_CONF)),
            cspec((1, D_CONF)), cspec((1, D_CONF)),
        ],
        out_specs=[
            pl.BlockSpec((None, T, D_XBC), row),
            pl.BlockSpec((None, T, D_SSM), row),
            pl.BlockSpec((None, T, LANES), row),
            pl.BlockSpec((None, T, D_CONF), row),
        ],
        out_shape=[
            jax.ShapeDtypeStruct((B, L, D_XBC), F32),
            jax.ShapeDtypeStruct((B, L, D_SSM), F32),
            jax.ShapeDtypeStruct((B, L, LANES), F32),
            jax.ShapeDtypeStruct((B, L, D_CONF), BF16),
        ],
        scratch_shapes=[
            pltpu.VMEM((SSM_HALO + T, D_XBC), F32),
            pltpu.VMEM((CONF_HALO + T, D_CONF), F32),
            pltpu.VMEM((SUBLANES - 1, T + CONF_SHIFT_ROWS, D_CONF), F32),
            pltpu.VMEM((T, D_CONF), F32),
            pltpu.VMEM((D_MODEL, W_COLS), BF16),
            pltpu.VMEM((2, W_PREP_ROWS, D_IN_PROJ), F32),
            pltpu.SemaphoreType.DMA((2,)),
        ],
        compiler_params=pltpu.CompilerParams(
            dimension_semantics=("arbitrary", "arbitrary"),
            vmem_limit_bytes=VMEM_LIMIT),
        name="front",
    )(x, ln_g, ln_b, w_in, scw, scb, dtb, bga, bgg, ccw, ccb, clg, clb)


def _ssd_kernel(xbc_ref, dt_ref, z_ref, alog_ref, dskip_ref, g_ref,
                expand_ref, tril_ref, y_out, state_ref):
    @pl.when(pl.program_id(1) == 0)
    def _():
        state_ref[...] = jnp.zeros(state_ref.shape, F32)

    for ci in range(SSD_CHUNKS_PER_STEP):
        rows = pl.ds(ci * CHUNK, CHUNK)
        _ssd_chunk(xbc_ref.at[rows], dt_ref.at[rows], z_ref.at[rows], alog_ref, dskip_ref,
                   g_ref, expand_ref, tril_ref, y_out.at[rows], state_ref)


def _ssd_chunk(xbc_ref, dt_ref, z_ref, alog_ref, dskip_ref, g_ref,
               expand_ref, tril_ref, y_out, state_ref):
    expand = expand_ref[...]
    tril = tril_ref[...]
    dt = dt_ref[...]
    a_small = -jnp.exp(alog_ref[...])
    lane = lax.broadcasted_iota(jnp.int32, (1, LANES), 1)
    adt = jnp.where(lane < N_HEADS, dt * a_small, 0.0)

    a_cs = None
    for piece in _split_bf16(adt, 3):
        d = _bdot(tril, piece)
        a_cs = d if a_cs is None else a_cs + d
    a_cs_t = a_cs.T

    def expand_heads(v, parts):
        out = None
        for piece in _split_bf16(v, parts):
            d = _bdot(piece, expand)
            out = d if out is None else out + d
        return out

    a_cs_x = expand_heads(a_cs, 3)
    dt_x = expand_heads(dt, 2)
    a_last_x = a_cs_x[CHUNK - 1:CHUNK, :]

    xs = xbc_ref[:, 0:D_SSM]
    xdt = xs * dt_x
    xdt_b = xdt.astype(BF16)
    xdt_decay_b = (xdt * jnp.exp(a_last_x - a_cs_x)).astype(BF16)
    exp_a_cs_x = jnp.exp(a_cs_x)

    ri = lax.broadcasted_iota(jnp.int32, (CHUNK, CHUNK), 0)
    ci = lax.broadcasted_iota(jnp.int32, (CHUNK, CHUNK), 1)
    causal = ri >= ci
    lane_lo = lax.broadcasted_iota(jnp.int32, (CHUNK, LANES), 1) < HEAD_DIM

    y_groups = []
    for g in range(N_GROUPS):
        glo = g * GROUP_W
        bm = xbc_ref[:, D_SSM + g * D_STATE:D_SSM + (g + 1) * D_STATE]
        cm = xbc_ref[:, D_SSM + N_GROUPS * D_STATE + g * D_STATE:
                     D_SSM + N_GROUPS * D_STATE + (g + 1) * D_STATE]
        bm_b = bm.astype(BF16)
        cm_b = cm.astype(BF16)
        cb = lax.dot_general(cm_b, bm_b, (((1,), (1,)), ((), ())),
                             preferred_element_type=F32)

        y_pairs = []
        for q in range(GROUP_W // LANES):
            lo = glo + q * LANES
            ms = []
            for e in range(2):
                j = (lo // HEAD_DIM) + e
                col = jnp.broadcast_to(a_cs[:, j:j + 1], (CHUNK, CHUNK))
                rowv = jnp.broadcast_to(a_cs_t[j:j + 1, :], (CHUNK, CHUNK))
                ldec = jnp.exp(jnp.where(causal, col - rowv, -jnp.inf))
                ms.append((cb * ldec).astype(BF16))
            m_pair = jnp.concatenate(ms, axis=1)
            xp = xdt_b[:, lo:lo + LANES]
            zero = jnp.zeros_like(xp)
            rhs = jnp.concatenate([jnp.where(lane_lo, xp, zero),
                                   jnp.where(lane_lo, zero, xp)], axis=0)
            y_pairs.append(_bdot(m_pair, rhs))
        y_diag = jnp.concatenate(y_pairs, axis=1)

        prev = state_ref[:, glo:glo + GROUP_W]
        y_off = _bdot(cm_b, prev.astype(BF16)) * exp_a_cs_x[:, glo:glo + GROUP_W]

        st = lax.dot_general(bm_b, xdt_decay_b[:, glo:glo + GROUP_W],
                             (((0,), (0,)), ((), ())), preferred_element_type=F32)
        state_ref[:, glo:glo + GROUP_W] = prev * jnp.exp(a_last_x[:, glo:glo + GROUP_W]) + st

        y = y_diag + y_off + dskip_ref[:, glo:glo + GROUP_W] * xs[:, glo:glo + GROUP_W]
        yz = y * _silu(z_ref[:, glo:glo + GROUP_W])
        ms2 = jnp.mean(yz * yz, axis=-1, keepdims=True)
        y_groups.append(yz * lax.rsqrt(ms2 + RMS_EPS) * g_ref[:, glo:glo + GROUP_W])
    y_out[...] = jnp.concatenate(y_groups, axis=1).astype(BF16)


def _ssd(xbc, dt, z, alog, dskip, norm_g, expand, tril):
    B, L, _ = xbc.shape
    row = lambda b, c: (b, c, 0)
    const = lambda b, c: (0, 0)
    T = SSD_CHUNKS_PER_STEP * CHUNK
    return pl.pallas_call(
        _ssd_kernel,
        grid=(B, L // T),
        in_specs=[
            pl.BlockSpec((None, T, D_XBC), row),
            pl.BlockSpec((None, T, LANES), row),
            pl.BlockSpec((None, T, D_SSM), row),
            pl.BlockSpec((1, LANES), const),
            pl.BlockSpec((1, D_SSM), const),
            pl.BlockSpec((1, D_SSM), const),
            pl.BlockSpec((LANES, D_SSM), const),
            pl.BlockSpec((CHUNK, CHUNK), const),
        ],
        out_specs=pl.BlockSpec((None, T, D_SSM), row),
        out_shape=jax.ShapeDtypeStruct((B, L, D_SSM), BF16),
        scratch_shapes=[pltpu.VMEM((D_STATE, D_SSM), F32)],
        compiler_params=pltpu.CompilerParams(
            dimension_semantics=("arbitrary", "arbitrary"),
            vmem_limit_bytes=VMEM_LIMIT),
        name="ssd",
    )(xbc, dt, z, alog, dskip, norm_g, expand, tril)


def _back_kernel(x_ref, p_ref, yssm_ref, yconf_ref, lng_ref, lnb_ref,
                 wo_s_ref, wo_c_ref, bo_ref, l1g_ref, l1b_ref,
                 wg_ref, wp_ref, l2g_ref, l2b_ref, out_ref):
    T = x_ref.shape[0]
    for r0 in range(0, T, BACK_ROWS):
        rows = slice(r0, r0 + BACK_ROWS)
        h = _layer_norm(x_ref[rows, :], lng_ref[...], lnb_ref[...])
        out = (_bdot(yssm_ref[rows, :], wo_s_ref[...]) + _bdot(yconf_ref[rows, :], wo_c_ref[...])
               + bo_ref[...])
        h1 = _layer_norm(ALPHA * h + out, l1g_ref[...], l1b_ref[...])
        gate = _sigmoid(_bdot(h1.astype(BF16), wg_ref[...]))
        ple = _bdot(p_ref[rows, :].astype(BF16), wp_ref[...])
        out_ref[rows, :] = _layer_norm(ALPHA * h1 + gate * ple, l2g_ref[...], l2b_ref[...])


def _back(x2, p2, yssm, yconf, ln_g, ln_b, wo_s, wo_c, bo, l1g, l1b, wg, wp, l2g, l2b):
    M = x2.shape[0]
    T = BACK_T
    row = lambda i: (i, 0)
    const = lambda i: (0, 0)

    def cspec(shape):
        return pl.BlockSpec(shape, const)

    return pl.pallas_call(
        _back_kernel,
        grid=(M // T,),
        in_specs=[
            pl.BlockSpec((T, D_MODEL), row),
            pl.BlockSpec((T, D_PLE), row),
            pl.BlockSpec((T, D_SSM), row),
            pl.BlockSpec((T, D_CONF), row),
            cspec((1, D_MODEL)), cspec((1, D_MODEL)),
            cspec((D_SSM, D_MODEL)), cspec((D_CONF, D_MODEL)), cspec((1, D_MODEL)),
            cspec((1, D_MODEL)), cspec((1, D_MODEL)),
            cspec((D_MODEL, D_MODEL)), cspec((D_PLE, D_MODEL)),
            cspec((1, D_MODEL)), cspec((1, D_MODEL)),
        ],
        out_specs=pl.BlockSpec((T, D_MODEL), row),
        out_shape=jax.ShapeDtypeStruct((M, D_MODEL), F32),
        compiler_params=pltpu.CompilerParams(
            dimension_semantics=("arbitrary",),
            vmem_limit_bytes=VMEM_LIMIT),
        name="back",
    )(x2, p2, yssm, yconf, ln_g, ln_b, wo_s, wo_c, bo, l1g, l1b, wg, wp, l2g, l2b)


def _row(v):
    return v.reshape(1, -1).astype(F32)


def _tile_rows(v):
    v = v.astype(F32)
    return jnp.broadcast_to(v[..., None, :], v.shape[:-1] + (SUBLANES, v.shape[-1]))


def kernel(x, p, ln_emb_g, ln_emb_b, w_in, ssm_conv_w, ssm_conv_b, dt_bias, a_log, d_skip,
           ssm_norm_g, b_glu, conf_conv_w, conf_conv_b, conf_ln_g, conf_ln_b, w_out, b_out,
           ln1_g, ln1_b, w_ple_gate, w_ple_proj, ln2_g, ln2_b):
    B, L, _ = x.shape
    pad16 = lambda v: jnp.concatenate([v.astype(F32), jnp.zeros((LANES - N_HEADS,), F32)]).reshape(1, LANES)

    xbc, z, dt, yconf = _front(
        x, _row(ln_emb_g), _row(ln_emb_b), w_in.astype(F32),
        _tile_rows(ssm_conv_w[0]), _tile_rows(ssm_conv_b[0]), pad16(dt_bias[0]),
        _row(b_glu[0, :D_CONF]), _row(b_glu[0, D_CONF:]),
        _tile_rows(conf_conv_w[0]), _tile_rows(conf_conv_b[0]),
        _row(conf_ln_g[0]), _row(conf_ln_b[0]))

    head_of_lane = np.arange(D_SSM) // HEAD_DIM
    expand = jnp.asarray(np.arange(LANES)[:, None] == head_of_lane[None, :], dtype=BF16)
    tril = jnp.asarray(np.tril(np.ones((CHUNK, CHUNK), np.float32)), dtype=BF16)
    yssm = _ssd(xbc, dt, z, pad16(a_log[0]),
                _row(jnp.repeat(d_skip[0], HEAD_DIM)), _row(ssm_norm_g[0]), expand, tril)

    wo = w_out[0].astype(BF16)
    out = _back(
        x.reshape(B * L, D_MODEL), p.reshape(B * L, D_PLE),
        yssm.reshape(B * L, D_SSM), yconf.reshape(B * L, D_CONF),
        _row(ln_emb_g), _row(ln_emb_b), wo[:D_SSM], wo[D_SSM:], _row(b_out[0]),
        _row(ln1_g[0]), _row(ln1_b[0]),
        w_ple_gate[0].astype(BF16), w_ple_proj[0].astype(BF16),
        _row(ln2_g[0]), _row(ln2_b[0]))
    return out.reshape(B, L, D_MODEL)
```

```python
import functools

import jax
import jax.numpy as jnp
import numpy as np
from jax import lax
from jax.experimental import pallas as pl
from jax.experimental.pallas import tpu as pltpu

F32 = jnp.float32
BF16 = jnp.bfloat16

D_MODEL = 1024
D_PLE = 256
D_SSM = 1024
D_CONF = 1024
HEAD_DIM = 64
N_HEADS = 16
N_GROUPS = 2
GROUP_W = D_SSM // N_GROUPS
D_STATE = 128
SSM_K = 4
CONF_K = 31
CHUNK = 128
D_XBC = D_SSM + 2 * N_GROUPS * D_STATE
LN_EPS = 1e-5
RMS_EPS = 1e-5
ALPHA = 2.0 ** 0.25

LANES = 128
SUBLANES = 8
VMEM_LIMIT = 56 * 1024 * 1024

C_X = 0
C_Z = C_X + D_SSM
C_BC = C_Z + D_SSM
C_DT = C_BC + 2 * N_GROUPS * D_STATE
C_GLU_A = C_DT + N_HEADS
C_GLU_G = C_GLU_A + D_CONF
C_CG = C_GLU_G + D_CONF

SSM_HALO = SUBLANES
CONF_HALO = 32
CONF_SHIFT_ROWS = CONF_HALO - SUBLANES
CONF_ROWS = 32
SSD_CHUNKS_PER_STEP = 4

FRONT_T = 256
D_IN_PROJ = 2 * D_SSM + 2 * N_GROUPS * D_STATE + N_HEADS + 3 * D_CONF
W_PREP_ROWS = 512
BACK_T = 1024
BACK_ROWS = 512


def _layer_norm(x, g, b):
    mu = jnp.mean(x, axis=-1, keepdims=True)
    xc = x - mu
    var = jnp.mean(xc * xc, axis=-1, keepdims=True)
    return xc * lax.rsqrt(var + LN_EPS) * g + b


def _sigmoid(x):
    return 0.5 * jnp.tanh(0.5 * x) + 0.5


def _silu(x):
    hx = 0.5 * x
    return hx * jnp.tanh(hx) + hx


def _softplus(x):
    return jnp.maximum(x, 0.0) + jnp.log1p(jnp.exp(-jnp.abs(x)))


def _bdot(a, b):
    return jnp.dot(a, b, preferred_element_type=F32)


def _split_bf16(x, parts):
    out = []
    r = x
    for i in range(parts):
        p = r.astype(BF16)
        out.append(p)
        if i + 1 < parts:
            r = r - p.astype(F32)
    return out


def _load_in_proj_weight(wt_hbm, w_ref, stage, sem):
    bounds = list(range(0, D_IN_PROJ, W_PREP_ROWS)) + [D_IN_PROJ]
    chunks = list(zip(bounds[:-1], bounds[1:]))

    def chunk_copy(c):
        lo, hi = chunks[c]
        return pltpu.make_async_copy(wt_hbm.at[pl.ds(lo, hi - lo), :],
                                     stage.at[c % 2, pl.ds(0, hi - lo), :], sem.at[c % 2])

    chunk_copy(0).start()
    for c, (lo, hi) in enumerate(chunks):
        if c + 1 < len(chunks):
            chunk_copy(c + 1).start()
        chunk_copy(c).wait()
        w_ref[lo:hi, :] = stage[c % 2, 0:hi - lo, :].astype(BF16)


def _proj(hb, w_ref, lo, hi):
    return lax.dot_general(hb, w_ref[lo:hi, :], (((1,), (1,)), ((), ())),
                           preferred_element_type=F32)


def _front_kernel(x_ref, lng_ref, lnb_ref, w_hbm, scw_ref, scb_ref, dtb_ref,
                  bga_ref, bgg_ref, ccw_ref, ccb_ref, clg_ref, clb_ref,
                  xbc_out, z_out, dt_out, yconf_out, sbuf, cbuf, pbuf, cg_ref, w_ref, stage, sem):
    t = pl.program_id(1)
    T = x_ref.shape[0]

    @pl.when(jnp.logical_and(pl.program_id(0) == 0, t == 0))
    def _():
        _load_in_proj_weight(w_hbm, w_ref, stage, sem)

    h = _layer_norm(x_ref[...], lng_ref[...], lnb_ref[...])
    hb = h.astype(BF16)

    @pl.when(t == 0)
    def _():
        sbuf[0:SSM_HALO, :] = jnp.zeros((SSM_HALO, D_XBC), F32)
        cbuf[0:CONF_HALO, :] = jnp.zeros((CONF_HALO, D_CONF), F32)

    @pl.when(t > 0)
    def _():
        sbuf[0:SSM_HALO, :] = sbuf[T:T + SSM_HALO, :]
        cbuf[0:CONF_HALO, :] = cbuf[T:T + CONF_HALO, :]

    ga = _proj(hb, w_ref, C_GLU_A, C_GLU_A + D_CONF) + bga_ref[...]
    gg = _proj(hb, w_ref, C_GLU_G, C_GLU_G + D_CONF) + bgg_ref[...]
    cbuf[CONF_HALO:CONF_HALO + T, :] = ga * _sigmoid(gg)
    for r in range(1, SUBLANES):
        pbuf[r - 1] = cbuf[r:r + T + CONF_SHIFT_ROWS, :]
    cg_ref[...] = _silu(_proj(hb, w_ref, C_CG, C_CG + D_CONF))
    sbuf[SSM_HALO:SSM_HALO + T, 0:D_SSM] = _proj(hb, w_ref, C_X, C_X + D_SSM)
    sbuf[SSM_HALO:SSM_HALO + T, D_SSM:D_XBC] = _proj(hb, w_ref, C_BC, C_DT)
    z_out[...] = _proj(hb, w_ref, C_Z, C_Z + D_SSM)
    dt_out[...] = _softplus(_proj(hb, w_ref, C_DT, C_DT + LANES) + dtb_ref[...])

    base = CONF_HALO - (CONF_K - 1)
    tiles = CONF_ROWS // SUBLANES
    for r0 in range(0, T, CONF_ROWS):
        accs = [ccb_ref[...]] * tiles
        for k in range(CONF_K):
            a, r = divmod(base + k, SUBLANES)
            wk = ccw_ref[k]
            for i in range(tiles):
                lo = r0 + (a + i) * SUBLANES
                if r == 0:
                    tap = cbuf[lo:lo + SUBLANES, :]
                else:
                    tap = pbuf[r - 1, lo:lo + SUBLANES, :]
                accs[i] = accs[i] + wk * tap
        acc = jnp.concatenate(accs, axis=0)
        u = _silu(_layer_norm(acc, clg_ref[...], clb_ref[...]))
        yconf_out[r0:r0 + CONF_ROWS, :] = (u * cg_ref[r0:r0 + CONF_ROWS, :]).astype(BF16)

    base = SSM_HALO - (SSM_K - 1)
    for r0 in range(0, T, SUBLANES):
        acc = scb_ref[...]
        for k in range(SSM_K):
            acc = acc + scw_ref[k] * sbuf[base + k + r0:base + k + r0 + SUBLANES, :]
        xbc_out[r0:r0 + SUBLANES, :] = _silu(acc)


def _front(x, ln_g, ln_b, w_in, scw, scb, dtb, bga, bgg, ccw, ccb, clg, clb):
    B, L, _ = x.shape
    T = FRONT_T
    row = lambda b, t: (b, t, 0)
    const = lambda b, t: (0, 0)

    def cspec(shape):
        return pl.BlockSpec(shape, const)

    return pl.pallas_call(
        _front_kernel,
        grid=(B, L // T),
        in_specs=[
            pl.BlockSpec((None, T, D_MODEL), row),
            cspec((1, D_MODEL)), cspec((1, D_MODEL)),
            pl.BlockSpec(memory_space=pl.ANY),
            pl.BlockSpec((SSM_K, SUBLANES, D_XBC), lambda b, t: (0, 0, 0)),
            cspec((SUBLANES, D_XBC)), cspec((1, LANES)),
            cspec((1, D_CONF)), cspec((1, D_CONF)),
            pl.BlockSpec((CONF_K, SUBLANES, D_CONF), lambda b, t: (0, 0, 0)),
            cspec((SUBLANES, D_CONF)),
            cspec((1, D_CONF)), cspec((1, D_CONF)),
        ],
        out_specs=[
            pl.BlockSpec((None, T, D_XBC), row),
            pl.BlockSpec((None, T, D_SSM), row),
            pl.BlockSpec((None, T, LANES), row),
            pl.BlockSpec((None, T, D_CONF), row),
        ],
        out_shape=[
            jax.ShapeDtypeStruct((B, L, D_XBC), F32),
            jax.ShapeDtypeStruct((B, L, D_SSM), F32),
            jax.ShapeDtypeStruct((B, L, LANES), F32),
            jax.ShapeDtypeStruct((B, L, D_CONF), BF16),
        ],
        scratch_shapes=[
            pltpu.VMEM((SSM_HALO + T, D_XBC), F32),
            pltpu.VMEM((CONF_HALO + T, D_CONF), F32),
            pltpu.VMEM((SUBLANES - 1, T + CONF_SHIFT_ROWS, D_CONF), F32),
            pltpu.VMEM((T, D_CONF), F32),
            pltpu.VMEM((D_IN_PROJ, D_MODEL), BF16),
            pltpu.VMEM((2, W_PREP_ROWS, D_MODEL), F32),
            pltpu.SemaphoreType.DMA((2,)),
        ],
        compiler_params=pltpu.CompilerParams(
            dimension_semantics=("arbitrary", "arbitrary"),
            vmem_limit_bytes=VMEM_LIMIT),
        name="front",
    )(x, ln_g, ln_b, w_in, scw, scb, dtb, bga, bgg, ccw, ccb, clg, clb)


def _ssd_kernel(xbc_ref, dt_ref, z_ref, alog_ref, dskip_ref, g_ref,
                expand_ref, tril_ref, y_out, state_ref):
    @pl.when(pl.program_id(1) == 0)
    def _():
        state_ref[...] = jnp.zeros(state_ref.shape, F32)

    for ci in range(SSD_CHUNKS_PER_STEP):
        rows = pl.ds(ci * CHUNK, CHUNK)
        _ssd_chunk(xbc_ref.at[rows], dt_ref.at[rows], z_ref.at[rows], alog_ref, dskip_ref,
                   g_ref, expand_ref, tril_ref, y_out.at[rows], state_ref)


def _ssd_chunk(xbc_ref, dt_ref, z_ref, alog_ref, dskip_ref, g_ref,
               expand_ref, tril_ref, y_out, state_ref):
    expand = expand_ref[...]
    tril = tril_ref[...]
    dt = dt_ref[...]
    a_small = -jnp.exp(alog_ref[...])
    lane = lax.broadcasted_iota(jnp.int32, (1, LANES), 1)
    adt = jnp.where(lane < N_HEADS, dt * a_small, 0.0)

    a_cs = None
    for piece in _split_bf16(adt, 3):
        d = _bdot(tril, piece)
        a_cs = d if a_cs is None else a_cs + d
    a_cs_t = a_cs.T

    def expand_heads(v, parts):
        out = None
        for piece in _split_bf16(v, parts):
            d = _bdot(piece, expand)
            out = d if out is None else out + d
        return out

    a_cs_x = expand_heads(a_cs, 3)
    dt_x = expand_heads(dt, 2)
    a_last_x = a_cs_x[CHUNK - 1:CHUNK, :]

    xs = xbc_ref[:, 0:D_SSM]
    xdt = xs * dt_x
    xdt_b = xdt.astype(BF16)
    xdt_decay_b = (xdt * jnp.exp(a_last_x - a_cs_x)).astype(BF16)
    exp_a_cs_x = jnp.exp(a_cs_x)

    ri = lax.broadcasted_iota(jnp.int32, (CHUNK, CHUNK), 0)
    ci = lax.broadcasted_iota(jnp.int32, (CHUNK, CHUNK), 1)
    causal = ri >= ci
    lane_lo = lax.broadcasted_iota(jnp.int32, (CHUNK, LANES), 1) < HEAD_DIM

    y_groups = []
    for g in range(N_GROUPS):
        glo = g * GROUP_W
        bm = xbc_ref[:, D_SSM + g * D_STATE:D_SSM + (g + 1) * D_STATE]
        cm = xbc_ref[:, D_SSM + N_GROUPS * D_STATE + g * D_STATE:
                     D_SSM + N_GROUPS * D_STATE + (g + 1) * D_STATE]
        bm_b = bm.astype(BF16)
        cm_b = cm.astype(BF16)
        cb = lax.dot_general(cm_b, bm_b, (((1,), (1,)), ((), ())),
                             preferred_element_type=F32)

        y_pairs = []
        for q in range(GROUP_W // LANES):
            lo = glo + q * LANES
            ms = []
            for e in range(2):
                j = (lo // HEAD_DIM) + e
                col = jnp.broadcast_to(a_cs[:, j:j + 1], (CHUNK, CHUNK))
                rowv = jnp.broadcast_to(a_cs_t[j:j + 1, :], (CHUNK, CHUNK))
                ldec = jnp.exp(jnp.where(causal, col - rowv, -jnp.inf))
                ms.append((cb * ldec).astype(BF16))
            m_pair = jnp.concatenate(ms, axis=1)
            xp = xdt_b[:, lo:lo + LANES]
            zero = jnp.zeros_like(xp)
            rhs = jnp.concatenate([jnp.where(lane_lo, xp, zero),
                                   jnp.where(lane_lo, zero, xp)], axis=0)
            y_pairs.append(_bdot(m_pair, rhs))
        y_diag = jnp.concatenate(y_pairs, axis=1)

        prev = state_ref[:, glo:glo + GROUP_W]
        y_off = _bdot(cm_b, prev.astype(BF16)) * exp_a_cs_x[:, glo:glo + GROUP_W]

        st = lax.dot_general(bm_b, xdt_decay_b[:, glo:glo + GROUP_W],
                             (((0,), (0,)), ((), ())), preferred_element_type=F32)
        state_ref[:, glo:glo + GROUP_W] = prev * jnp.exp(a_last_x[:, glo:glo + GROUP_W]) + st

        y = y_diag + y_off + dskip_ref[:, glo:glo + GROUP_W] * xs[:, glo:glo + GROUP_W]
        yz = y * _silu(z_ref[:, glo:glo + GROUP_W])
        ms2 = jnp.mean(yz * yz, axis=-1, keepdims=True)
        y_groups.append(yz * lax.rsqrt(ms2 + RMS_EPS) * g_ref[:, glo:glo + GROUP_W])
    y_out[...] = jnp.concatenate(y_groups, axis=1).astype(BF16)


def _ssd(xbc, dt, z, alog, dskip, norm_g, expand, tril):
    B, L, _ = xbc.shape
    row = lambda b, c: (b, c, 0)
    const = lambda b, c: (0, 0)
    T = SSD_CHUNKS_PER_STEP * CHUNK
    return pl.pallas_call(
        _ssd_kernel,
        grid=(B, L // T),
        in_specs=[
            pl.BlockSpec((None, T, D_XBC), row),
            pl.BlockSpec((None, T, LANES), row),
            pl.BlockSpec((None, T, D_SSM), row),
            pl.BlockSpec((1, LANES), const),
            pl.BlockSpec((1, D_SSM), const),
            pl.BlockSpec((1, D_SSM), const),
            pl.BlockSpec((LANES, D_SSM), const),
            pl.BlockSpec((CHUNK, CHUNK), const),
        ],
        out_specs=pl.BlockSpec((None, T, D_SSM), row),
        out_shape=jax.ShapeDtypeStruct((B, L, D_SSM), BF16),
        scratch_shapes=[pltpu.VMEM((D_STATE, D_SSM), F32)],
        compiler_params=pltpu.CompilerParams(
            dimension_semantics=("arbitrary", "arbitrary"),
            vmem_limit_bytes=VMEM_LIMIT),
        name="ssd",
    )(xbc, dt, z, alog, dskip, norm_g, expand, tril)


def _back_kernel(x_ref, p_ref, yssm_ref, yconf_ref, lng_ref, lnb_ref,
                 wo_s_ref, wo_c_ref, bo_ref, l1g_ref, l1b_ref,
                 wg_ref, wp_ref, l2g_ref, l2b_ref, out_ref):
    T = x_ref.shape[0]
    for r0 in range(0, T, BACK_ROWS):
        rows = slice(r0, r0 + BACK_ROWS)
        h = _layer_norm(x_ref[rows, :], lng_ref[...], lnb_ref[...])
        out = (_bdot(yssm_ref[rows, :], wo_s_ref[...]) + _bdot(yconf_ref[rows, :], wo_c_ref[...])
               + bo_ref[...])
        h1 = _layer_norm(ALPHA * h + out, l1g_ref[...], l1b_ref[...])
        gate = _sigmoid(_bdot(h1.astype(BF16), wg_ref[...]))
        ple = _bdot(p_ref[rows, :].astype(BF16), wp_ref[...])
        out_ref[rows, :] = _layer_norm(ALPHA * h1 + gate * ple, l2g_ref[...], l2b_ref[...])


def _back(x2, p2, yssm, yconf, ln_g, ln_b, wo_s, wo_c, bo, l1g, l1b, wg, wp, l2g, l2b):
    M = x2.shape[0]
    T = BACK_T
    row = lambda i: (i, 0)
    const = lambda i: (0, 0)

    def cspec(shape):
        return pl.BlockSpec(shape, const)

    return pl.pallas_call(
        _back_kernel,
        grid=(M // T,),
        in_specs=[
            pl.BlockSpec((T, D_MODEL), row),
            pl.BlockSpec((T, D_PLE), row),
            pl.BlockSpec((T, D_SSM), row),
            pl.BlockSpec((T, D_CONF), row),
            cspec((1, D_MODEL)), cspec((1, D_MODEL)),
            cspec((D_SSM, D_MODEL)), cspec((D_CONF, D_MODEL)), cspec((1, D_MODEL)),
            cspec((1, D_MODEL)), cspec((1, D_MODEL)),
            cspec((D_MODEL, D_MODEL)), cspec((D_PLE, D_MODEL)),
            cspec((1, D_MODEL)), cspec((1, D_MODEL)),
        ],
        out_specs=pl.BlockSpec((T, D_MODEL), row),
        out_shape=jax.ShapeDtypeStruct((M, D_MODEL), F32),
        compiler_params=pltpu.CompilerParams(
            dimension_semantics=("arbitrary",),
            vmem_limit_bytes=VMEM_LIMIT),
        name="back",
    )(x2, p2, yssm, yconf, ln_g, ln_b, wo_s, wo_c, bo, l1g, l1b, wg, wp, l2g, l2b)


def _row(v):
    return v.reshape(1, -1).astype(F32)


def _tile_rows(v):
    v = v.astype(F32)
    return jnp.broadcast_to(v[..., None, :], v.shape[:-1] + (SUBLANES, v.shape[-1]))


def kernel(x, p, ln_emb_g, ln_emb_b, w_in, ssm_conv_w, ssm_conv_b, dt_bias, a_log, d_skip,
           ssm_norm_g, b_glu, conf_conv_w, conf_conv_b, conf_ln_g, conf_ln_b, w_out, b_out,
           ln1_g, ln1_b, w_ple_gate, w_ple_proj, ln2_g, ln2_b):
    B, L, _ = x.shape
    pad16 = lambda v: jnp.concatenate([v.astype(F32), jnp.zeros((LANES - N_HEADS,), F32)]).reshape(1, LANES)

    xbc, z, dt, yconf = _front(
        x, _row(ln_emb_g), _row(ln_emb_b), w_in[0].astype(F32).T,
        _tile_rows(ssm_conv_w[0]), _tile_rows(ssm_conv_b[0]), pad16(dt_bias[0]),
        _row(b_glu[0, :D_CONF]), _row(b_glu[0, D_CONF:]),
        _tile_rows(conf_conv_w[0]), _tile_rows(conf_conv_b[0]),
        _row(conf_ln_g[0]), _row(conf_ln_b[0]))

    head_of_lane = np.arange(D_SSM) // HEAD_DIM
    expand = jnp.asarray(np.arange(LANES)[:, None] == head_of_lane[None, :], dtype=BF16)
    tril = jnp.asarray(np.tril(np.ones((CHUNK, CHUNK), np.float32)), dtype=BF16)
    yssm = _ssd(xbc, dt, z, pad16(a_log[0]),
                _row(jnp.repeat(d_skip[0], HEAD_DIM)), _row(ssm_norm_g[0]), expand, tril)

    wo = w_out[0].astype(BF16)
    out = _back(
        x.reshape(B * L, D_MODEL), p.reshape(B * L, D_PLE),
        yssm.reshape(B * L, D_SSM), yconf.reshape(B * L, D_CONF),
        _row(ln_emb_g), _row(ln_emb_b), wo[:D_SSM], wo[D_SSM:], _row(b_out[0]),
        _row(ln1_g[0]), _row(ln1_b[0]),
        w_ple_gate[0].astype(BF16), w_ple_proj[0].astype(BF16),
        _row(ln2_g[0]), _row(ln2_b[0]))
    return out.reshape(B, L, D_MODEL)
```

```python
import functools

import jax
import jax.numpy as jnp
import numpy as np
from jax import lax
from jax.experimental import pallas as pl
from jax.experimental.pallas import tpu as pltpu

F32 = jnp.float32
BF16 = jnp.bfloat16

D_MODEL = 1024
D_PLE = 256
D_SSM = 1024
D_CONF = 1024
HEAD_DIM = 64
N_HEADS = 16
N_GROUPS = 2
GROUP_W = D_SSM // N_GROUPS
D_STATE = 128
SSM_K = 4
CONF_K = 31
CHUNK = 128
D_XBC = D_SSM + 2 * N_GROUPS * D_STATE
LN_EPS = 1e-5
RMS_EPS = 1e-5
ALPHA = 2.0 ** 0.25

LANES = 128
SUBLANES = 8
VMEM_LIMIT = 56 * 1024 * 1024

C_XBC = 0
C_Z = C_XBC + D_XBC
C_GLU_A = C_Z + D_SSM
C_GLU_G = C_GLU_A + D_CONF
C_CG = C_GLU_G + D_CONF
C_DT = C_CG + D_CONF
W_COLS = C_DT + LANES

SSM_HALO = SUBLANES
CONF_HALO = 32
CONF_SHIFT_ROWS = CONF_HALO - SUBLANES
CONF_ROWS = 32
SSD_CHUNKS_PER_STEP = 4

FRONT_T = 256
D_IN_PROJ = 2 * D_SSM + 2 * N_GROUPS * D_STATE + N_HEADS + 3 * D_CONF
W_PREP_ROWS = 512
BACK_T = 1024
BACK_ROWS = 512


def _layer_norm(x, g, b):
    mu = jnp.mean(x, axis=-1, keepdims=True)
    xc = x - mu
    var = jnp.mean(xc * xc, axis=-1, keepdims=True)
    return xc * lax.rsqrt(var + LN_EPS) * g + b


def _sigmoid(x):
    return 0.5 * jnp.tanh(0.5 * x) + 0.5


def _silu(x):
    hx = 0.5 * x
    return hx * jnp.tanh(hx) + hx


def _softplus(x):
    return jnp.maximum(x, 0.0) + jnp.log1p(jnp.exp(-jnp.abs(x)))


def _bdot(a, b):
    return jnp.dot(a, b, preferred_element_type=F32)


def _split_bf16(x, parts):
    out = []
    r = x
    for i in range(parts):
        p = r.astype(BF16)
        out.append(p)
        if i + 1 < parts:
            r = r - p.astype(F32)
    return out


def _in_proj_weight_plan():
    gn = N_GROUPS * D_STATE
    o_b = 2 * D_SSM
    o_dt = o_b + 2 * gn
    o_ga = o_dt + N_HEADS
    parts = [(0, D_SSM, C_XBC), (o_b, 2 * gn, C_XBC + D_SSM), (D_SSM, D_SSM, C_Z),
             (o_ga, 3 * D_CONF, C_GLU_A), (o_dt, N_HEADS, C_DT)]
    plan = []
    for src, n, dst in parts:
        for off in range(0, n, W_PREP_ROWS):
            plan.append((src + off, min(W_PREP_ROWS, n - off), dst + off))
    return plan


def _load_in_proj_weight(wt_hbm, w_ref, stage, sem):
    plan = _in_proj_weight_plan()

    def piece_copy(i):
        src, n, _ = plan[i]
        return pltpu.make_async_copy(wt_hbm.at[pl.ds(src, n), :],
                                     stage.at[i % 2, pl.ds(0, n), :], sem.at[i % 2])

    piece_copy(0).start()
    for i, (src, n, dst) in enumerate(plan):
        if i + 1 < len(plan):
            piece_copy(i + 1).start()
        piece_copy(i).wait()
        if n == W_PREP_ROWS:
            w_ref[:, dst:dst + n] = stage[i % 2].T.astype(BF16)
        else:
            rows = lax.broadcasted_iota(jnp.int32, (LANES, D_MODEL), 0)
            blk = jnp.where(rows < n, stage[i % 2, 0:LANES, :], 0.0)
            w_ref[:, dst:dst + LANES] = blk.T.astype(BF16)


def _front_kernel(x_ref, lng_ref, lnb_ref, w_hbm, scw_ref, scb_ref, dtb_ref,
                  bga_ref, bgg_ref, ccw_ref, ccb_ref, clg_ref, clb_ref,
                  xbc_out, z_out, dt_out, yconf_out, sbuf, cbuf, pbuf, cg_ref, w_ref, stage, sem):
    t = pl.program_id(1)
    T = x_ref.shape[0]

    @pl.when(jnp.logical_and(pl.program_id(0) == 0, t == 0))
    def _():
        _load_in_proj_weight(w_hbm, w_ref, stage, sem)

    h = _layer_norm(x_ref[...], lng_ref[...], lnb_ref[...])
    hb = h.astype(BF16)

    @pl.when(t == 0)
    def _():
        sbuf[0:SSM_HALO, :] = jnp.zeros((SSM_HALO, D_XBC), F32)
        cbuf[0:CONF_HALO, :] = jnp.zeros((CONF_HALO, D_CONF), F32)

    @pl.when(t > 0)
    def _():
        sbuf[0:SSM_HALO, :] = sbuf[T:T + SSM_HALO, :]
        cbuf[0:CONF_HALO, :] = cbuf[T:T + CONF_HALO, :]

    ga = _bdot(hb, w_ref[:, C_GLU_A:C_GLU_A + D_CONF]) + bga_ref[...]
    gg = _bdot(hb, w_ref[:, C_GLU_G:C_GLU_G + D_CONF]) + bgg_ref[...]
    cbuf[CONF_HALO:CONF_HALO + T, :] = ga * _sigmoid(gg)
    for r in range(1, SUBLANES):
        pbuf[r - 1] = cbuf[r:r + T + CONF_SHIFT_ROWS, :]
    cg_ref[...] = _silu(_bdot(hb, w_ref[:, C_CG:C_CG + D_CONF]))
    sbuf[SSM_HALO:SSM_HALO + T, :] = _bdot(hb, w_ref[:, C_XBC:C_XBC + D_XBC])
    z_out[...] = _bdot(hb, w_ref[:, C_Z:C_Z + D_SSM])
    dt_out[...] = _softplus(_bdot(hb, w_ref[:, C_DT:C_DT + LANES]) + dtb_ref[...])

    base = CONF_HALO - (CONF_K - 1)
    tiles = CONF_ROWS // SUBLANES
    for r0 in range(0, T, CONF_ROWS):
        accs = [ccb_ref[...]] * tiles
        for k in range(CONF_K):
            a, r = divmod(base + k, SUBLANES)
            wk = ccw_ref[k]
            for i in range(tiles):
                lo = r0 + (a + i) * SUBLANES
                if r == 0:
                    tap = cbuf[lo:lo + SUBLANES, :]
                else:
                    tap = pbuf[r - 1, lo:lo + SUBLANES, :]
                accs[i] = accs[i] + wk * tap
        acc = jnp.concatenate(accs, axis=0)
        u = _silu(_layer_norm(acc, clg_ref[...], clb_ref[...]))
        yconf_out[r0:r0 + CONF_ROWS, :] = (u * cg_ref[r0:r0 + CONF_ROWS, :]).astype(BF16)

    base = SSM_HALO - (SSM_K - 1)
    for r0 in range(0, T, SUBLANES):
        acc = scb_ref[...]
        for k in range(SSM_K):
            acc = acc + scw_ref[k] * sbuf[base + k + r0:base + k + r0 + SUBLANES, :]
        xbc_out[r0:r0 + SUBLANES, :] = _silu(acc)


def _front(x, ln_g, ln_b, w_in, scw, scb, dtb, bga, bgg, ccw, ccb, clg, clb):
    B, L, _ = x.shape
    T = FRONT_T
    row = lambda b, t: (b, t, 0)
    const = lambda b, t: (0, 0)

    def cspec(shape):
        return pl.BlockSpec(shape, const)

    return pl.pallas_call(
        _front_kernel,
        grid=(B, L // T),
        in_specs=[
            pl.BlockSpec((None, T, D_MODEL), row),
            cspec((1, D_MODEL)), cspec((1, D_MODEL)),
            pl.BlockSpec(memory_space=pl.ANY),
            pl.BlockSpec((SSM_K, SUBLANES, D_XBC), lambda b, t: (0, 0, 0)),
            cspec((SUBLANES, D_XBC)), cspec((1, LANES)),
            cspec((1, D_CONF)), cspec((1, D_CONF)),
            pl.BlockSpec((CONF_K, SUBLANES, D_CONF), lambda b, t: (0, 0, 0)),
            cspec((SUBLANES, D_CONF)),
            cspec((1, D_CONF)), cspec((1, D_CONF)),
        ],
        out_specs=[
            pl.BlockSpec((None, T, D_XBC), row),
            pl.BlockSpec((None, T, D_SSM), row),
            pl.BlockSpec((None, T, LANES), row),
            pl.BlockSpec((None, T, D_CONF), row),
        ],
        out_shape=[
            jax.ShapeDtypeStruct((B, L, D_XBC), F32),
            jax.ShapeDtypeStruct((B, L, D_SSM), F32),
            jax.ShapeDtypeStruct((B, L, LANES), F32),
            jax.ShapeDtypeStruct((B, L, D_CONF), BF16),
        ],
        scratch_shapes=[
            pltpu.VMEM((SSM_HALO + T, D_XBC), F32),
            pltpu.VMEM((CONF_HALO + T, D_CONF), F32),
            pltpu.VMEM((SUBLANES - 1, T + CONF_SHIFT_ROWS, D_CONF), F32),
            pltpu.VMEM((T, D_CONF), F32),
            pltpu.VMEM((D_MODEL, W_COLS), BF16),
            pltpu.VMEM((2, W_PREP_ROWS, D_MODEL), F32),
            pltpu.SemaphoreType.DMA((2,)),
        ],
        compiler_params=pltpu.CompilerParams(
            dimension_semantics=("arbitrary", "arbitrary"),
            vmem_limit_bytes=VMEM_LIMIT),
        name="front",
    )(x, ln_g, ln_b, w_in, scw, scb, dtb, bga, bgg, ccw, ccb, clg, clb)


def _ssd_kernel(xbc_ref, dt_ref, z_ref, alog_ref, dskip_ref, g_ref,
                expand_ref, tril_ref, y_out, state_ref):
    @pl.when(pl.program_id(1) == 0)
    def _():
        state_ref[...] = jnp.zeros(state_ref.shape, F32)

    for ci in range(SSD_CHUNKS_PER_STEP):
        rows = pl.ds(ci * CHUNK, CHUNK)
        _ssd_chunk(xbc_ref.at[rows], dt_ref.at[rows], z_ref.at[rows], alog_ref, dskip_ref,
                   g_ref, expand_ref, tril_ref, y_out.at[rows], state_ref)


def _ssd_chunk(xbc_ref, dt_ref, z_ref, alog_ref, dskip_ref, g_ref,
               expand_ref, tril_ref, y_out, state_ref):
    expand = expand_ref[...]
    tril = tril_ref[...]
    dt = dt_ref[...]
    a_small = -jnp.exp(alog_ref[...])
    lane = lax.broadcasted_iota(jnp.int32, (1, LANES), 1)
    adt = jnp.where(lane < N_HEADS, dt * a_small, 0.0)

    a_cs = None
    for piece in _split_bf16(adt, 3):
        d = _bdot(tril, piece)
        a_cs = d if a_cs is None else a_cs + d
    a_cs_t = a_cs.T

    def expand_heads(v, parts):
        out = None
        for piece in _split_bf16(v, parts):
            d = _bdot(piece, expand)
            out = d if out is None else out + d
        return out

    a_cs_x = expand_heads(a_cs, 3)
    dt_x = expand_heads(dt, 2)
    a_last_x = a_cs_x[CHUNK - 1:CHUNK, :]

    xs = xbc_ref[:, 0:D_SSM]
    xdt = xs * dt_x
    xdt_b = xdt.astype(BF16)
    xdt_decay_b = (xdt * jnp.exp(a_last_x - a_cs_x)).astype(BF16)
    exp_a_cs_x = jnp.exp(a_cs_x)

    ri = lax.broadcasted_iota(jnp.int32, (CHUNK, CHUNK), 0)
    ci = lax.broadcasted_iota(jnp.int32, (CHUNK, CHUNK), 1)
    causal = ri >= ci
    lane_lo = lax.broadcasted_iota(jnp.int32, (CHUNK, LANES), 1) < HEAD_DIM

    y_groups = []
    for g in range(N_GROUPS):
        glo = g * GROUP_W
        bm = xbc_ref[:, D_SSM + g * D_STATE:D_SSM + (g + 1) * D_STATE]
        cm = xbc_ref[:, D_SSM + N_GROUPS * D_STATE + g * D_STATE:
                     D_SSM + N_GROUPS * D_STATE + (g + 1) * D_STATE]
        bm_b = bm.astype(BF16)
        cm_b = cm.astype(BF16)
        cb = lax.dot_general(cm_b, bm_b, (((1,), (1,)), ((), ())),
                             preferred_element_type=F32)

        y_pairs = []
        for q in range(GROUP_W // LANES):
            lo = glo + q * LANES
            ms = []
            for e in range(2):
                j = (lo // HEAD_DIM) + e
                col = jnp.broadcast_to(a_cs[:, j:j + 1], (CHUNK, CHUNK))
                rowv = jnp.broadcast_to(a_cs_t[j:j + 1, :], (CHUNK, CHUNK))
                ldec = jnp.exp(jnp.where(causal, col - rowv, -jnp.inf))
                ms.append((cb * ldec).astype(BF16))
            m_pair = jnp.concatenate(ms, axis=1)
            xp = xdt_b[:, lo:lo + LANES]
            zero = jnp.zeros_like(xp)
            rhs = jnp.concatenate([jnp.where(lane_lo, xp, zero),
                                   jnp.where(lane_lo, zero, xp)], axis=0)
            y_pairs.append(_bdot(m_pair, rhs))
        y_diag = jnp.concatenate(y_pairs, axis=1)

        prev = state_ref[:, glo:glo + GROUP_W]
        y_off = _bdot(cm_b, prev.astype(BF16)) * exp_a_cs_x[:, glo:glo + GROUP_W]

        st = lax.dot_general(bm_b, xdt_decay_b[:, glo:glo + GROUP_W],
                             (((0,), (0,)), ((), ())), preferred_element_type=F32)
        state_ref[:, glo:glo + GROUP_W] = prev * jnp.exp(a_last_x[:, glo:glo + GROUP_W]) + st

        y = y_diag + y_off + dskip_ref[:, glo:glo + GROUP_W] * xs[:, glo:glo + GROUP_W]
        yz = y * _silu(z_ref[:, glo:glo + GROUP_W])
        ms2 = jnp.mean(yz * yz, axis=-1, keepdims=True)
        y_groups.append(yz * lax.rsqrt(ms2 + RMS_EPS) * g_ref[:, glo:glo + GROUP_W])
    y_out[...] = jnp.concatenate(y_groups, axis=1).astype(BF16)


def _ssd(xbc, dt, z, alog, dskip, norm_g, expand, tril):
    B, L, _ = xbc.shape
    row = lambda b, c: (b, c, 0)
    const = lambda b, c: (0, 0)
    T = SSD_CHUNKS_PER_STEP * CHUNK
    return pl.pallas_call(
        _ssd_kernel,
        grid=(B, L // T),
        in_specs=[
            pl.BlockSpec((None, T, D_XBC), row),
            pl.BlockSpec((None, T, LANES), row),
            pl.BlockSpec((None, T, D_SSM), row),
            pl.BlockSpec((1, LANES), const),
            pl.BlockSpec((1, D_SSM), const),
            pl.BlockSpec((1, D_SSM), const),
            pl.BlockSpec((LANES, D_SSM), const),
            pl.BlockSpec((CHUNK, CHUNK), const),
        ],
        out_specs=pl.BlockSpec((None, T, D_SSM), row),
        out_shape=jax.ShapeDtypeStruct((B, L, D_SSM), BF16),
        scratch_shapes=[pltpu.VMEM((D_STATE, D_SSM), F32)],
        compiler_params=pltpu.CompilerParams(
            dimension_semantics=("arbitrary", "arbitrary"),
            vmem_limit_bytes=VMEM_LIMIT),
        name="ssd",
    )(xbc, dt, z, alog, dskip, norm_g, expand, tril)


def _back_kernel(x_ref, p_ref, yssm_ref, yconf_ref, lng_ref, lnb_ref,
                 wo_s_ref, wo_c_ref, bo_ref, l1g_ref, l1b_ref,
                 wg_ref, wp_ref, l2g_ref, l2b_ref, out_ref):
    T = x_ref.shape[0]
    for r0 in range(0, T, BACK_ROWS):
        rows = slice(r0, r0 + BACK_ROWS)
        h = _layer_norm(x_ref[rows, :], lng_ref[...], lnb_ref[...])
        out = (_bdot(yssm_ref[rows, :], wo_s_ref[...]) + _bdot(yconf_ref[rows, :], wo_c_ref[...])
               + bo_ref[...])
        h1 = _layer_norm(ALPHA * h + out, l1g_ref[...], l1b_ref[...])
        gate = _sigmoid(_bdot(h1.astype(BF16), wg_ref[...]))
        ple = _bdot(p_ref[rows, :].astype(BF16), wp_ref[...])
        out_ref[rows, :] = _layer_norm(ALPHA * h1 + gate * ple, l2g_ref[...], l2b_ref[...])


def _back(x2, p2, yssm, yconf, ln_g, ln_b, wo_s, wo_c, bo, l1g, l1b, wg, wp, l2g, l2b):
    M = x2.shape[0]
    T = BACK_T
    row = lambda i: (i, 0)
    const = lambda i: (0, 0)

    def cspec(shape):
        return pl.BlockSpec(shape, const)

    return pl.pallas_call(
        _back_kernel,
        grid=(M // T,),
        in_specs=[
            pl.BlockSpec((T, D_MODEL), row),
            pl.BlockSpec((T, D_PLE), row),
            pl.BlockSpec((T, D_SSM), row),
            pl.BlockSpec((T, D_CONF), row),
            cspec((1, D_MODEL)), cspec((1, D_MODEL)),
            cspec((D_SSM, D_MODEL)), cspec((D_CONF, D_MODEL)), cspec((1, D_MODEL)),
            cspec((1, D_MODEL)), cspec((1, D_MODEL)),
            cspec((D_MODEL, D_MODEL)), cspec((D_PLE, D_MODEL)),
            cspec((1, D_MODEL)), cspec((1, D_MODEL)),
        ],
        out_specs=pl.BlockSpec((T, D_MODEL), row),
        out_shape=jax.ShapeDtypeStruct((M, D_MODEL), F32),
        compiler_params=pltpu.CompilerParams(
            dimension_semantics=("arbitrary",),
            vmem_limit_bytes=VMEM_LIMIT),
        name="back",
    )(x2, p2, yssm, yconf, ln_g, ln_b, wo_s, wo_c, bo, l1g, l1b, wg, wp, l2g, l2b)


def _row(v):
    return v.reshape(1, -1).astype(F32)


def _tile_rows(v):
    v = v.astype(F32)
    return jnp.broadcast_to(v[..., None, :], v.shape[:-1] + (SUBLANES, v.shape[-1]))


def kernel(x, p, ln_emb_g, ln_emb_b, w_in, ssm_conv_w, ssm_conv_b, dt_bias, a_log, d_skip,
           ssm_norm_g, b_glu, conf_conv_w, conf_conv_b, conf_ln_g, conf_ln_b, w_out, b_out,
           ln1_g, ln1_b, w_ple_gate, w_ple_proj, ln2_g, ln2_b):
    B, L, _ = x.shape
    pad16 = lambda v: jnp.concatenate([v.astype(F32), jnp.zeros((LANES - N_HEADS,), F32)]).reshape(1, LANES)

    xbc, z, dt, yconf = _front(
        x, _row(ln_emb_g), _row(ln_emb_b), w_in[0].astype(F32).T,
        _tile_rows(ssm_conv_w[0]), _tile_rows(ssm_conv_b[0]), pad16(dt_bias[0]),
        _row(b_glu[0, :D_CONF]), _row(b_glu[0, D_CONF:]),
        _tile_rows(conf_conv_w[0]), _tile_rows(conf_conv_b[0]),
        _row(conf_ln_g[0]), _row(conf_ln_b[0]))

    head_of_lane = np.arange(D_SSM) // HEAD_DIM
    expand = jnp.asarray(np.arange(LANES)[:, None] == head_of_lane[None, :], dtype=BF16)
    tril = jnp.asarray(np.tril(np.ones((CHUNK, CHUNK), np.float32)), dtype=BF16)
    yssm = _ssd(xbc, dt, z, pad16(a_log[0]),
                _row(jnp.repeat(d_skip[0], HEAD_DIM)), _row(ssm_norm_g[0]), expand, tril)

    wo = w_out[0].astype(BF16)
    out = _back(
        x.reshape(B * L, D_MODEL), p.reshape(B * L, D_PLE),
        yssm.reshape(B * L, D_SSM), yconf.reshape(B * L, D_CONF),
        _row(ln_emb_g), _row(ln_emb_b), wo[:D_SSM], wo[D_SSM:], _row(b_out[0]),
        _row(ln1_g[0]), _row(ln1_b[0]),
        w_ple_gate[0].astype(BF16), w_ple_proj[0].astype(BF16),
        _row(ln2_g[0]), _row(ln2_b[0]))
    return out.reshape(B, L, D_MODEL)
```
